```python
import jax, jax.numpy as jnp
from jax import lax
import numpy as np

D_MODEL = 1024
BATCH = 8
SEQ = 2048
DEPTH = 4

HEAD_DIM = 64
N_HEADS = D_MODEL // HEAD_DIM
ATTN_WIDTH = N_HEADS * HEAD_DIM
SWA_KV_HEADS = max(1, N_HEADS // 8)
SWA_GROUP = N_HEADS // SWA_KV_HEADS
SWA_WINDOW = 128
ROPE_THETA = 500000.0
ROPE_DIM = HEAD_DIM // 4
D_FF = 4 * D_MODEL
PLE_DIM = 256
Q_BLOCK = 128
N_MIXERS = 3
RMS_EPS = 1e-6
NEG_INF = -1e30
IN_COLS = (3 * ATTN_WIDTH,
           ATTN_WIDTH + 2 * SWA_KV_HEADS * HEAD_DIM,
           3 * ATTN_WIDTH + N_HEADS)

kernel_name = "interleaved_sb_swa_fox_hybrid"


def rms_norm(x, g):
    xf = x.astype(jnp.float32)
    y = xf * lax.rsqrt(jnp.mean(xf * xf, axis=-1, keepdims=True) + RMS_EPS)
    return (y * g.astype(jnp.float32)).astype(x.dtype)


def partial_rope(x, pos):
    half = ROPE_DIM // 2
    inv_freq = ROPE_THETA ** (-jnp.arange(half, dtype=jnp.float32) / half)
    ang = pos.astype(jnp.float32)[:, None] * inv_freq[None, :]
    cos = jnp.cos(ang)[None, :, None, :]
    sin = jnp.sin(ang)[None, :, None, :]
    xr = x[..., :ROPE_DIM].astype(jnp.float32)
    x1, x2 = xr[..., :half], xr[..., half:]
    rot = jnp.concatenate([x1 * cos - x2 * sin, x2 * cos + x1 * sin], axis=-1).astype(x.dtype)
    return jnp.concatenate([rot, x[..., ROPE_DIM:]], axis=-1)


def stick_breaking_attention(h, w_in):
    B, S, _ = h.shape
    proj = h @ w_in
    q = proj[..., :ATTN_WIDTH].reshape(B, S, N_HEADS, HEAD_DIM)
    k = proj[..., ATTN_WIDTH:2 * ATTN_WIDTH].reshape(B, S, N_HEADS, HEAD_DIM)
    v = proj[..., 2 * ATTN_WIDTH:].reshape(B, S, N_HEADS, HEAD_DIM)
    scale = HEAD_DIM ** -0.5
    outs = []
    for blk in range(S // Q_BLOCK):
        q0, q1 = blk * Q_BLOCK, (blk + 1) * Q_BLOCK
        z = jnp.einsum('bqhd,bkhd->bhqk', q[:, q0:q1], k[:, :q1]).astype(jnp.float32) * scale
        t = q0 + jnp.arange(Q_BLOCK)
        s = jnp.arange(q1)
        strict = s[None, :] < t[:, None]
        log_beta = jax.nn.log_sigmoid(z)
        log_one_minus = jnp.where(strict, jax.nn.log_sigmoid(-z), 0.0)
        tail = lax.cumsum(log_one_minus, axis=3, reverse=True) - log_one_minus
        w = jnp.where(strict, jnp.exp(log_beta + tail), 0.0)
        outs.append(jnp.einsum('bhqk,bkhd->bqhd', w.astype(v.dtype), v[:, :q1]))
    return jnp.concatenate(outs, axis=1).reshape(B, S, ATTN_WIDTH)


def sliding_window_sink_attention(h, w_in, sinks, pos):
    B, S, _ = h.shape
    nkv = SWA_KV_HEADS * HEAD_DIM
    proj = h @ w_in
    q = proj[..., :ATTN_WIDTH].reshape(B, S, N_HEADS, HEAD_DIM)
    k = proj[..., ATTN_WIDTH:ATTN_WIDTH + nkv].reshape(B, S, SWA_KV_HEADS, HEAD_DIM)
    v = proj[..., ATTN_WIDTH + nkv:].reshape(B, S, SWA_KV_HEADS, HEAD_DIM)
    q = partial_rope(q, pos)
    k = partial_rope(k, pos)
    nb = S // Q_BLOCK
    pad = ((0, 0), (Q_BLOCK, 0), (0, 0), (0, 0))
    kp = jnp.pad(k, pad).reshape(B, nb + 1, Q_BLOCK, SWA_KV_HEADS, HEAD_DIM)
    vp = jnp.pad(v, pad).reshape(B, nb + 1, Q_BLOCK, SWA_KV_HEADS, HEAD_DIM)
    kb = jnp.concatenate([kp[:, :-1], kp[:, 1:]], axis=2)
    vb = jnp.concatenate([vp[:, :-1], vp[:, 1:]], axis=2)
    qb = q.reshape(B, nb, Q_BLOCK, SWA_KV_HEADS, SWA_GROUP, HEAD_DIM)
    logits = jnp.einsum('bnqkgd,bnckd->bnkgqc', qb, kb).astype(jnp.float32) * (HEAD_DIM ** -0.5)
    blk = jnp.arange(nb)[:, None, None]
    q_pos = blk * Q_BLOCK + jnp.arange(Q_BLOCK)[None, :, None]
    k_pos = blk * Q_BLOCK + jnp.arange(2 * Q_BLOCK)[None, None, :] - Q_BLOCK
    diff = q_pos - k_pos
    mask = (diff >= 0) & (diff < SWA_WINDOW) & (k_pos >= 0)
    logits = jnp.where(mask[None, :, None, None], logits, NEG_INF)
    sink = sinks.astype(jnp.float32).reshape(SWA_KV_HEADS, SWA_GROUP)[None, None, :, :, None, None]
    m = jnp.maximum(jnp.max(logits, axis=-1, keepdims=True), sink)
    e = jnp.exp(logits - m)
    probs = e / (jnp.sum(e, axis=-1, keepdims=True) + jnp.exp(sink - m))
    o = jnp.einsum('bnkgqc,bnckd->bnqkgd', probs.astype(vb.dtype), vb)
    return o.reshape(B, S, ATTN_WIDTH)


def forgetting_attention(h, w_in, b_forget):
    B, S, _ = h.shape
    proj = h @ w_in
    q = proj[..., :ATTN_WIDTH].reshape(B, S, N_HEADS, HEAD_DIM)
    k = proj[..., ATTN_WIDTH:2 * ATTN_WIDTH].reshape(B, S, N_HEADS, HEAD_DIM)
    v = proj[..., 2 * ATTN_WIDTH:3 * ATTN_WIDTH].reshape(B, S, N_HEADS, HEAD_DIM)
    log_f = jax.nn.log_sigmoid(proj[..., 3 * ATTN_WIDTH:].astype(jnp.float32)
                               + b_forget.astype(jnp.float32))
    cum = lax.cumsum(log_f, axis=1).transpose(0, 2, 1)
    scale = HEAD_DIM ** -0.5
    outs = []
    for blk in range(S // Q_BLOCK):
        q0, q1 = blk * Q_BLOCK, (blk + 1) * Q_BLOCK
        logits = jnp.einsum('bqhd,bkhd->bhqk', q[:, q0:q1], k[:, :q1]).astype(jnp.float32) * scale
        logits = logits + cum[:, :, q0:q1, None] - cum[:, :, None, :q1]
        t = q0 + jnp.arange(Q_BLOCK)
        s = jnp.arange(q1)
        causal = s[None, :] <= t[:, None]
        probs = jax.nn.softmax(jnp.where(causal, logits, NEG_INF), axis=-1)
        outs.append(jnp.einsum('bhqk,bkhd->bqhd', probs.astype(v.dtype), v[:, :q1]))
    return jnp.concatenate(outs, axis=1).reshape(B, S, ATTN_WIDTH)


def squared_relu_mlp(h, w_up, w_down):
    return jnp.square(jax.nn.relu(h @ w_up)) @ w_down


def per_layer_input(x, p_i, ple_norm, w_gate, w_proj):
    gate = jax.nn.sigmoid((rms_norm(x, ple_norm) @ w_gate).astype(jnp.float32)).astype(x.dtype)
    return (p_i @ w_proj) * gate


def setup_inputs(seed: int = 0) -> dict:
    key = jax.random.key(seed)
    keys = iter(jax.random.split(key, 64))

    def nrm(shape, scale):
        return jax.random.normal(next(keys), shape, jnp.float32) * scale

    def gain():
        return 1.0 + nrm((D_MODEL,), 0.02)

    inp = {"x": nrm((BATCH, SEQ, D_MODEL), 1.0),
           "p": nrm((DEPTH, BATCH, SEQ, PLE_DIM), 1.0)}
    for i in range(DEPTH):
        kind = i % N_MIXERS
        inp[f"attn_norm_{i}"] = gain()
        inp[f"w_in_{i}"] = nrm((D_MODEL, IN_COLS[kind]), D_MODEL ** -0.5)
        inp[f"w_out_{i}"] = nrm((ATTN_WIDTH, D_MODEL), ATTN_WIDTH ** -0.5)
        if kind == 1:
            inp[f"sinks_{i}"] = nrm((N_HEADS,), 0.5)
        if kind == 2:
            inp[f"b_forget_{i}"] = jax.random.uniform(next(keys), (N_HEADS,), jnp.float32, 1.0, 4.0)
        inp[f"mlp_norm_{i}"] = gain()
        inp[f"w_up_{i}"] = nrm((D_MODEL, D_FF), D_MODEL ** -0.5)
        inp[f"w_down_{i}"] = nrm((D_FF, D_MODEL), D_FF ** -0.5)
        inp[f"ple_norm_{i}"] = gain()
        inp[f"w_ple_gate_{i}"] = nrm((D_MODEL, D_MODEL), D_MODEL ** -0.5)
        inp[f"w_ple_proj_{i}"] = nrm((PLE_DIM, D_MODEL), PLE_DIM ** -0.5)
    inp["final_norm"] = gain()
    return inp


def reference(x, p,
              attn_norm_0, w_in_0, w_out_0, mlp_norm_0, w_up_0, w_down_0, ple_norm_0, w_ple_gate_0, w_ple_proj_0,
              attn_norm_1, w_in_1, w_out_1, sinks_1, mlp_norm_1, w_up_1, w_down_1, ple_norm_1, w_ple_gate_1, w_ple_proj_1,
              attn_norm_2, w_in_2, w_out_2, b_forget_2, mlp_norm_2, w_up_2, w_down_2, ple_norm_2, w_ple_gate_2, w_ple_proj_2,
              attn_norm_3, w_in_3, w_out_3, mlp_norm_3, w_up_3, w_down_3, ple_norm_3, w_ple_gate_3, w_ple_proj_3,
              final_norm):
    layers = [
        (attn_norm_0, w_in_0, w_out_0, None, mlp_norm_0, w_up_0, w_down_0, ple_norm_0, w_ple_gate_0, w_ple_proj_0),
        (attn_norm_1, w_in_1, w_out_1, sinks_1, mlp_norm_1, w_up_1, w_down_1, ple_norm_1, w_ple_gate_1, w_ple_proj_1),
        (attn_norm_2, w_in_2, w_out_2, b_forget_2, mlp_norm_2, w_up_2, w_down_2, ple_norm_2, w_ple_gate_2, w_ple_proj_2),
        (attn_norm_3, w_in_3, w_out_3, None, mlp_norm_3, w_up_3, w_down_3, ple_norm_3, w_ple_gate_3, w_ple_proj_3),
    ]
    pos = jnp.arange(x.shape[1], dtype=jnp.int32)
    for i in range(DEPTH):
        an, wi, wo, extra, mn, wu, wd, pn, wg, wp = layers[i]
        kind = i % N_MIXERS
        h = rms_norm(x, an)
        if kind == 0:
            a = stick_breaking_attention(h, wi)
        elif kind == 1:
            a = sliding_window_sink_attention(h, wi, extra, pos)
        else:
            a = forgetting_attention(h, wi, extra)
        x = x + a @ wo
        x = x + squared_relu_mlp(rms_norm(x, mn), wu, wd)
        x = x + per_layer_input(x, p[i], pn, wg, wp)
    return rms_norm(x, final_norm)
```

```python
import functools

import jax
import jax.numpy as jnp
import numpy as np
from jax import lax
from jax.experimental import pallas as pl
from jax.experimental.pallas import tpu as pltpu

D_MODEL = 1024
HEAD_DIM = 64
N_HEADS = 16
ATTN_WIDTH = N_HEADS * HEAD_DIM
SWA_KV_HEADS = 2
SWA_GROUP = N_HEADS // SWA_KV_HEADS
SWA_WINDOW = 128
ROPE_THETA = 500000.0
ROPE_DIM = 16
D_FF = 4 * D_MODEL
PLE_DIM = 256
Q_BLOCK = 128
RMS_EPS = 1e-6
NEG_INF = -1e30
SCALE = HEAD_DIM ** -0.5

LANES = 128
HEADS_PER_TILE = LANES // HEAD_DIM
N_HEAD_TILES = ATTN_WIDTH // LANES
TOKEN_TILE = 512
FF_CHUNK = 1024
VMEM_LIMIT = 56 * 1024 * 1024

F32 = jnp.float32
BF16 = jnp.bfloat16


def _params(n_axes, parallel=True):
    sem = ("parallel" if parallel else "arbitrary",) * n_axes
    return pltpu.CompilerParams(dimension_semantics=sem, vmem_limit_bytes=VMEM_LIMIT)


def _const_spec(shape):
    nd = len(shape)
    return pl.BlockSpec(shape, lambda *_: (0,) * nd, pipeline_mode=pl.Buffered(1))


def _rms(x, g):
    ms = jnp.mean(x * x, axis=-1, keepdims=True)
    return (x * lax.rsqrt(ms + RMS_EPS)) * g


def _dot(a, b):
    return jnp.dot(a, b, preferred_element_type=F32)


def _dot_t(a, b):
    return lax.dot_general(a, b, (((1,), (1,)), ((), ())), preferred_element_type=F32)


def _head_masks(rows=Q_BLOCK):
    lane = lax.broadcasted_iota(jnp.int32, (rows, LANES), 1)
    m0 = jnp.where(lane < HEAD_DIM, 1.0, 0.0).astype(BF16)
    m1 = jnp.where(lane < HEAD_DIM, 0.0, 1.0).astype(BF16)
    return m0, m1


def _inproj_kernel(x_ref, g_ref, w_ref, o_ref, h_ref, *, n_out):
    h_ref[...] = _rms(x_ref[...], g_ref[...]).astype(BF16)
    for c in range(0, n_out, FF_CHUNK):
        o_ref[:, c:c + FF_CHUNK] = _dot(h_ref[...], w_ref[:, c:c + FF_CHUNK]).astype(BF16)


def _inproj(x2d, g, w):
    t, d = x2d.shape
    n_out = w.shape[1]
    return pl.pallas_call(
        functools.partial(_inproj_kernel, n_out=n_out),
        grid=(t // TOKEN_TILE,),
        in_specs=[pl.BlockSpec((TOKEN_TILE, d), lambda i: (i, 0)),
                  _const_spec((1, d)), _const_spec((d, n_out))],
        out_specs=pl.BlockSpec((TOKEN_TILE, n_out), lambda i: (i, 0)),
        out_shape=jax.ShapeDtypeStruct((t, n_out), BF16),
        scratch_shapes=[pltpu.VMEM((TOKEN_TILE, d), BF16)],
        compiler_params=_params(1),
        name="inproj",
    )(x2d, g, w)


def _inproj_fox_kernel(x_ref, g_ref, w_ref, wf_ref, o_ref, f_ref, h_ref, *, n_out):
    h_ref[...] = _rms(x_ref[...], g_ref[...]).astype(BF16)
    for c in range(0, n_out, FF_CHUNK):
        o_ref[:, c:c + FF_CHUNK] = _dot(h_ref[...], w_ref[:, c:c + FF_CHUNK]).astype(BF16)
    f_ref[...] = _dot(h_ref[...], wf_ref[...])


def _inproj_fox(x2d, g, w, wf):
    t, d = x2d.shape
    n_out = w.shape[1]
    return pl.pallas_call(
        functools.partial(_inproj_fox_kernel, n_out=n_out),
        grid=(t // TOKEN_TILE,),
        in_specs=[pl.BlockSpec((TOKEN_TILE, d), lambda i: (i, 0)),
                  _const_spec((1, d)), _const_spec((d, n_out)), _const_spec((d, LANES))],
        out_specs=[pl.BlockSpec((TOKEN_TILE, n_out), lambda i: (i, 0)),
                   pl.BlockSpec((TOKEN_TILE, LANES), lambda i: (i, 0))],
        out_shape=[jax.ShapeDtypeStruct((t, n_out), BF16),
                   jax.ShapeDtypeStruct((t, LANES), F32)],
        scratch_shapes=[pltpu.VMEM((TOKEN_TILE, d), BF16)],
        compiler_params=_params(1),
        name="inproj_fox",
    )(x2d, g, w, wf)


def _inproj_swa_kernel(x_ref, g_ref, w_ref, c_ref, s1_ref, s2_ref, o_ref, h_ref, *, n_rope, n_out):
    h_ref[...] = _rms(x_ref[...], g_ref[...]).astype(BF16)
    cos, s1, s2 = c_ref[...], s1_ref[...], s2_ref[...]
    for c in range(0, n_out, LANES):
        y = _dot(h_ref[...], w_ref[:, c:c + LANES])
        if c < n_rope:
            y = y * cos + pltpu.roll(y, 8, 1) * s1 + pltpu.roll(y, LANES - 8, 1) * s2
        o_ref[:, c:c + LANES] = y.astype(BF16)


def _inproj_swa(x2d, g, w, cos, s1, s2, seq):
    t, d = x2d.shape
    n_out = w.shape[1]
    n_rope = ATTN_WIDTH + SWA_KV_HEADS * HEAD_DIM
    tiles_per_seq = seq // TOKEN_TILE
    tab = pl.BlockSpec((TOKEN_TILE, LANES), lambda i: (i % tiles_per_seq, 0))
    return pl.pallas_call(
        functools.partial(_inproj_swa_kernel, n_rope=n_rope, n_out=n_out),
        grid=(t // TOKEN_TILE,),
        in_specs=[pl.BlockSpec((TOKEN_TILE, d), lambda i: (i, 0)),
                  _const_spec((1, d)), _const_spec((d, n_out)), tab, tab, tab],
        out_specs=pl.BlockSpec((TOKEN_TILE, n_out), lambda i: (i, 0)),
        out_shape=jax.ShapeDtypeStruct((t, n_out), BF16),
        scratch_shapes=[pltpu.VMEM((TOKEN_TILE, d), BF16)],
        compiler_params=_params(1),
        name="inproj_swa",
    )(x2d, g, w, cos, s1, s2)


def _rope_tables(seq):
    half = ROPE_DIM // 2
    inv_freq = ROPE_THETA ** (-jnp.arange(half, dtype=F32) / half)
    ang = jnp.arange(seq, dtype=jnp.int32).astype(F32)[:, None] * inv_freq[None, :]
    cos, sin = jnp.cos(ang), jnp.sin(ang)
    one = jnp.ones((seq, HEAD_DIM - ROPE_DIM), F32)
    zero = jnp.zeros((seq, HEAD_DIM - ROPE_DIM), F32)
    zh = jnp.zeros((seq, half), F32)
    c_head = jnp.concatenate([cos, cos, one], axis=1)
    s1_head = jnp.concatenate([zh, sin, zero], axis=1)
    s2_head = jnp.concatenate([-sin, zh, zero], axis=1)
    tile = lambda a: jnp.concatenate([a] * HEADS_PER_TILE, axis=1)
    return tile(c_head), tile(s1_head), tile(s2_head)


def _sb_kernel(q_ref, k_ref, v_ref, uu_ref, o_ref, acc_ref, r_ref, *, n_blocks):
    sel = _head_masks()
    row = lax.broadcasted_iota(jnp.int32, (Q_BLOCK, Q_BLOCK), 0)
    col = lax.broadcasted_iota(jnp.int32, (Q_BLOCK, Q_BLOCK), 1)
    strict = col < row

    def q_block(i, _):
        qi = q_ref[pl.ds(pl.multiple_of(i * Q_BLOCK, Q_BLOCK), Q_BLOCK), :] * jnp.asarray(SCALE, BF16)
        qh = (qi * sel[0], qi * sel[1])
        acc_ref[...] = jnp.zeros_like(acc_ref)
        r_ref[...] = jnp.zeros_like(r_ref)

        def pair(j, diagonal):
            off = pl.multiple_of(j * Q_BLOCK, Q_BLOCK)
            kj = k_ref[pl.ds(off, Q_BLOCK), :]
            vj = v_ref[pl.ds(off, Q_BLOCK), :]
            ws = []
            for h in range(HEADS_PER_TILE):
                z = _dot_t(qh[h], kj)
                sp = jnp.maximum(z, 0.0) + jnp.log1p(jnp.exp(-jnp.abs(z)))
                lom = -sp
                if diagonal:
                    lom = jnp.where(strict, lom, 0.0)
                hi = lom.astype(BF16)
                lo = (lom - hi.astype(F32)).astype(BF16)
                cs = _dot(jnp.concatenate([hi, lo], axis=1), uu_ref[...])
                r = r_ref[h]
                tail = cs[:, :Q_BLOCK] + r
                r_ref[h] = r + cs[:, Q_BLOCK:]
                w = jnp.exp((z - sp) + tail)
                if diagonal:
                    w = jnp.where(strict, w, 0.0)
                ws.append(w.astype(BF16))
            vstack = jnp.concatenate([vj * sel[0], vj * sel[1]], axis=0)
            acc_ref[...] += _dot(jnp.concatenate(ws, axis=1), vstack)

        pair(i, True)

        def off_diag(t, _):
            pair(i - 1 - t, False)
            return 0

        lax.fori_loop(0, i, off_diag, 0)
        o_ref[pl.ds(pl.multiple_of(i * Q_BLOCK, Q_BLOCK), Q_BLOCK), :] = acc_ref[...].astype(BF16)
        return 0

    lax.fori_loop(0, n_blocks, q_block, 0)


def _cumsum_weights():
    a = np.arange(Q_BLOCK)
    u = (a[:, None] > a[None, :]).astype(np.float32)
    half = np.concatenate([u, np.ones_like(u)], axis=1)
    return jnp.asarray(np.concatenate([half, half], axis=0), BF16)


def _sb_attention(qkv, batch, seq):
    blk = lambda off: pl.BlockSpec((None, seq, LANES), lambda b, t: (b, 0, off + t))
    return pl.pallas_call(
        functools.partial(_sb_kernel, n_blocks=seq // Q_BLOCK),
        grid=(batch, N_HEAD_TILES),
        in_specs=[blk(0), blk(N_HEAD_TILES), blk(2 * N_HEAD_TILES),
                  _const_spec((2 * Q_BLOCK, 2 * Q_BLOCK))],
        out_specs=pl.BlockSpec((None, seq, LANES), lambda b, t: (b, 0, t)),
        out_shape=jax.ShapeDtypeStruct((batch, seq, ATTN_WIDTH), BF16),
        scratch_shapes=[pltpu.VMEM((Q_BLOCK, LANES), F32),
                        pltpu.VMEM((HEADS_PER_TILE, Q_BLOCK, Q_BLOCK), F32)],
        compiler_params=_params(2),
        name="sb_attention",
    )(qkv, qkv, qkv, _cumsum_weights())


def _fox_cum_kernel(f_ref, b_ref, tri_ref, o_ref, *, n_blocks):
    x = f_ref[...] + b_ref[...]
    lf = jnp.minimum(x, 0.0) - jnp.log1p(jnp.exp(-jnp.abs(x)))
    lft = lf.T
    carry = jnp.zeros((LANES, 1), F32)
    for c in range(n_blocks):
        blk = lft[:, c * Q_BLOCK:(c + 1) * Q_BLOCK]
        hi = blk.astype(BF16)
        r1 = blk - hi.astype(F32)
        mid = r1.astype(BF16)
        lo = (r1 - mid.astype(F32)).astype(BF16)
        cs = _dot(hi, tri_ref[...]) + _dot(mid, tri_ref[...]) + _dot(lo, tri_ref[...]) + carry
        o_ref[:, c * Q_BLOCK:(c + 1) * Q_BLOCK] = cs[:N_HEADS]
        carry = cs[:, Q_BLOCK - 1:Q_BLOCK]


def _fox_cum(flog, b_pad, batch, seq):
    a = np.arange(Q_BLOCK)
    tri = jnp.asarray((a[:, None] <= a[None, :]).astype(np.float32), BF16)
    return pl.pallas_call(
        functools.partial(_fox_cum_kernel, n_blocks=seq // Q_BLOCK),
        grid=(batch,),
        in_specs=[pl.BlockSpec((None, seq, LANES), lambda b: (b, 0, 0)),
                  _const_spec((1, LANES)), _const_spec((Q_BLOCK, Q_BLOCK))],
        out_specs=pl.BlockSpec((None, N_HEADS, seq), lambda b: (b, 0, 0)),
        out_shape=jax.ShapeDtypeStruct((batch, N_HEADS, seq), F32),
        compiler_params=_params(1),
        name="fox_cum",
    )(flog, b_pad, tri)


def _fox_kernel(q_ref, k_ref, v_ref, c_ref, o_ref, acc_ref, l_ref, m_ref, *, n_blocks):
    lane = lax.broadcasted_iota(jnp.int32, (Q_BLOCK, LANES), 1)
    head0 = lane < HEAD_DIM
    row = lax.broadcasted_iota(jnp.int32, (Q_BLOCK, Q_BLOCK), 0)
    col = lax.broadcasted_iota(jnp.int32, (Q_BLOCK, Q_BLOCK), 1)
    causal = col <= row
    sel = _head_masks()

    def q_block(i, _):
        qi = q_ref[pl.ds(pl.multiple_of(i * Q_BLOCK, Q_BLOCK), Q_BLOCK), :] * jnp.asarray(SCALE, BF16)
        qh = (qi * sel[0], qi * sel[1])
        acc_ref[...] = jnp.zeros_like(acc_ref)
        l_ref[...] = jnp.zeros_like(l_ref)
        m_ref[...] = jnp.full_like(m_ref, NEG_INF)

        def pair(j, diagonal):
            off = pl.multiple_of(j * Q_BLOCK, Q_BLOCK)
            kj = k_ref[pl.ds(off, Q_BLOCK), :]
            vj = v_ref[pl.ds(off, Q_BLOCK), :]
            ps, alphas = [], []
            for h in range(HEADS_PER_TILE):
                cj = c_ref[h, :, pl.ds(off, Q_BLOCK)]
                s = _dot_t(qh[h], kj) - cj
                if diagonal:
                    s = jnp.where(causal, s, NEG_INF)
                m_old = m_ref[h]
                m_new = jnp.maximum(m_old, jnp.max(s, axis=-1, keepdims=True))
                m_ref[h] = m_new
                alphas.append(jnp.exp(m_old - m_new))
                ps.append(jnp.exp(s - m_new).astype(BF16))
            alpha = jnp.where(head0, alphas[0], alphas[1])
            vext = jnp.concatenate(
                [jnp.concatenate([vj * sel[0], sel[0]], axis=1),
                 jnp.concatenate([vj * sel[1], sel[1]], axis=1)], axis=0)
            pv = _dot(jnp.concatenate(ps, axis=1), vext)
            acc_ref[...] = acc_ref[...] * alpha + pv[:, :LANES]
            l_ref[...] = l_ref[...] * alpha + pv[:, LANES:]

        pair(i, True)

        def off_diag(t, _):
            pair(i - 1 - t, False)
            return 0

        lax.fori_loop(0, i, off_diag, 0)
        o_ref[pl.ds(pl.multiple_of(i * Q_BLOCK, Q_BLOCK), Q_BLOCK), :] = (
            acc_ref[...] / l_ref[...]).astype(BF16)
        return 0

    lax.fori_loop(0, n_blocks, q_block, 0)


def _fox_attention(qkv, cum, batch, seq):
    blk = lambda off: pl.BlockSpec((None, seq, LANES), lambda b, t: (b, 0, off + t))
    return pl.pallas_call(
        functools.partial(_fox_kernel, n_blocks=seq // Q_BLOCK),
        grid=(batch, N_HEAD_TILES),
        in_specs=[blk(0), blk(N_HEAD_TILES), blk(2 * N_HEAD_TILES),
                  pl.BlockSpec((None, HEADS_PER_TILE, 1, seq), lambda b, t: (b, t, 0, 0))],
        out_specs=pl.BlockSpec((None, seq, LANES), lambda b, t: (b, 0, t)),
        out_shape=jax.ShapeDtypeStruct((batch, seq, ATTN_WIDTH), BF16),
        scratch_shapes=[pltpu.VMEM((Q_BLOCK, LANES), F32),
                        pltpu.VMEM((Q_BLOCK, LANES), F32),
                        pltpu.VMEM((HEADS_PER_TILE, Q_BLOCK, 1), F32)],
        compiler_params=_params(2),
        name="fox_attention",
    )(qkv, qkv, qkv, cum)


def _swa_kernel(sink_ref, q_ref, kp_ref, kc_ref, vp_ref, vc_ref, o_ref):
    g = pl.program_id(1)
    n = pl.program_id(2)
    lane = lax.broadcasted_iota(jnp.int32, (2 * Q_BLOCK, LANES), 1)
    kw = jnp.concatenate([kp_ref[...], kc_ref[...]], axis=0).astype(F32)
    vw = jnp.concatenate([vp_ref[...], vc_ref[...]], axis=0).astype(F32)
    mine = (lane >= g * HEAD_DIM) & (lane < (g + 1) * HEAD_DIM)
    kg = jnp.where(mine, kw, 0.0)
    vg = jnp.where(mine, vw, 0.0)
    k2 = (kg + pltpu.roll(kg, HEAD_DIM, 1)).astype(BF16)
    v2 = (vg + pltpu.roll(vg, HEAD_DIM, 1)).astype(BF16)

    a = lax.broadcasted_iota(jnp.int32, (Q_BLOCK, 2 * Q_BLOCK), 0)
    c = lax.broadcasted_iota(jnp.int32, (Q_BLOCK, 2 * Q_BLOCK), 1)
    mask = (c > a) & (c <= a + SWA_WINDOW) & (c + n * Q_BLOCK >= Q_BLOCK)
    qlane = lax.broadcasted_iota(jnp.int32, (Q_BLOCK, LANES), 1)
    q_h0 = qlane < HEAD_DIM
    sel = _head_masks()

    for t in range(SWA_GROUP // HEADS_PER_TILE):
        qt = q_ref[:, t * LANES:(t + 1) * LANES] * jnp.asarray(SCALE, BF16)
        outs = []
        for h in range(HEADS_PER_TILE):
            s = jnp.where(mask, _dot_t(qt * sel[h], k2), NEG_INF)
            sink = sink_ref[g * SWA_GROUP + t * HEADS_PER_TILE + h]
            m = jnp.maximum(jnp.max(s, axis=-1, keepdims=True), sink)
            e = jnp.exp(s - m)
            den = jnp.sum(e, axis=-1, keepdims=True) + jnp.exp(sink - m)
            outs.append(_dot((e / den).astype(BF16), v2))
        o_ref[:, t * LANES:(t + 1) * LANES] = jnp.where(q_h0, outs[0], outs[1]).astype(BF16)


def _swa_attention(qkv, sinks, batch, seq):
    nb = seq // Q_BLOCK
    gw = SWA_GROUP * HEAD_DIM
    k_tile = ATTN_WIDTH // LANES
    v_tile = k_tile + 1
    prev = lambda tile: pl.BlockSpec((None, Q_BLOCK, LANES),
                                     lambda b, g, n, s: (b, jnp.maximum(n - 1, 0), tile))
    cur = lambda tile: pl.BlockSpec((None, Q_BLOCK, LANES), lambda b, g, n, s: (b, n, tile))
    grid_spec = pltpu.PrefetchScalarGridSpec(
        num_scalar_prefetch=1,
        grid=(batch, SWA_KV_HEADS, nb),
        in_specs=[pl.BlockSpec((None, Q_BLOCK, gw), lambda b, g, n, s: (b, n, g)),
                  prev(k_tile), cur(k_tile), prev(v_tile), cur(v_tile)],
        out_specs=pl.BlockSpec((None, Q_BLOCK, gw), lambda b, g, n, s: (b, n, g)),
    )
    return pl.pallas_call(
        _swa_kernel,
        grid_spec=grid_spec,
        out_shape=jax.ShapeDtypeStruct((batch, seq, ATTN_WIDTH), BF16),
        compiler_params=_params(3),
        name="swa_attention",
    )(sinks, qkv, qkv, qkv, qkv, qkv)


def _tail_kernel(x_ref, a_ref, p_ref, wo_ref, mg_ref, wu_ref, wd_ref, pg_ref, wg_ref, wp_ref,
                 fg_ref, o_ref, h_ref, *, final):
    x1 = x_ref[...] + _dot(a_ref[...], wo_ref[...])
    h_ref[...] = _rms(x1, mg_ref[...]).astype(BF16)
    x2 = x1
    for c in range(0, D_FF, FF_CHUNK):
        u = jnp.maximum(_dot(h_ref[...], wu_ref[:, c:c + FF_CHUNK]), 0.0)
        x2 = x2 + _dot((u * u).astype(BF16), wd_ref[c:c + FF_CHUNK, :])
    h3 = _rms(x2, pg_ref[...]).astype(BF16)
    gate = 1.0 / (1.0 + jnp.exp(-_dot(h3, wg_ref[...])))
    x3 = x2 + _dot(p_ref[...].astype(BF16), wp_ref[...]) * gate
    o_ref[...] = _rms(x3, fg_ref[...]) if final else x3


def _tail(x2d, a2d, p2d, wo, mg, wu, wd, pg, wg, wp, fg, final):
    t, d = x2d.shape
    tok = lambda w: pl.BlockSpec((TOKEN_TILE, w), lambda i: (i, 0))
    return pl.pallas_call(
        functools.partial(_tail_kernel, final=final),
        grid=(t // TOKEN_TILE,),
        in_specs=[tok(d), tok(ATTN_WIDTH), tok(PLE_DIM),
                  _const_spec((ATTN_WIDTH, d)), _const_spec((1, d)),
                  _const_spec((d, D_FF)), _const_spec((D_FF, d)), _const_spec((1, d)),
                  _const_spec((d, d)), _const_spec((PLE_DIM, d)), _const_spec((1, d))],
        out_specs=tok(d),
        out_shape=jax.ShapeDtypeStruct((t, d), F32),
        scratch_shapes=[pltpu.VMEM((TOKEN_TILE, d), BF16)],
        compiler_params=_params(1),
        name="layer_tail",
    )(x2d, a2d, p2d, wo, mg, wu, wd, pg, wg, wp, fg)


def kernel(x, p, attn_norm_0, w_in_0, w_out_0, mlp_norm_0, w_up_0, w_down_0, ple_norm_0, w_ple_gate_0, w_ple_proj_0, attn_norm_1, w_in_1, w_out_1, sinks_1, mlp_norm_1, w_up_1, w_down_1, ple_norm_1, w_ple_gate_1, w_ple_proj_1, attn_norm_2, w_in_2, w_out_2, b_forget_2, mlp_norm_2, w_up_2, w_down_2, ple_norm_2, w_ple_gate_2, w_ple_proj_2, attn_norm_3, w_in_3, w_out_3, mlp_norm_3, w_up_3, w_down_3, ple_norm_3, w_ple_gate_3, w_ple_proj_3, final_norm):
    batch, seq, d = x.shape
    t = batch * seq
    layers = [
        (attn_norm_0, w_in_0, w_out_0, None, mlp_norm_0, w_up_0, w_down_0, ple_norm_0, w_ple_gate_0, w_ple_proj_0),
        (attn_norm_1, w_in_1, w_out_1, sinks_1, mlp_norm_1, w_up_1, w_down_1, ple_norm_1, w_ple_gate_1, w_ple_proj_1),
        (attn_norm_2, w_in_2, w_out_2, b_forget_2, mlp_norm_2, w_up_2, w_down_2, ple_norm_2, w_ple_gate_2, w_ple_proj_2),
        (attn_norm_3, w_in_3, w_out_3, None, mlp_norm_3, w_up_3, w_down_3, ple_norm_3, w_ple_gate_3, w_ple_proj_3),
    ]
    row = lambda g: g.reshape(1, d)
    bf = lambda w: w.astype(BF16)
    x2d = x.reshape(t, d)
    fg = row(final_norm)
    for i, (an, wi, wo, extra, mn, wu, wd, pn, wg, wp) in enumerate(layers):
        kind = i % 3
        if kind == 0:
            qkv = _inproj(x2d, row(an), bf(wi))
            a = _sb_attention(qkv.reshape(batch, seq, -1), batch, seq)
        elif kind == 1:
            cos, s1, s2 = _rope_tables(seq)
            qkv = _inproj_swa(x2d, row(an), bf(wi), cos, s1, s2, seq)
            a = _swa_attention(qkv.reshape(batch, seq, -1), extra.astype(F32), batch, seq)
        else:
            wf = jnp.pad(wi[:, 3 * ATTN_WIDTH:], ((0, 0), (0, LANES - N_HEADS)))
            qkv, flog = _inproj_fox(x2d, row(an), bf(wi[:, :3 * ATTN_WIDTH]), bf(wf))
            b_pad = jnp.pad(extra.astype(F32), (0, LANES - N_HEADS)).reshape(1, LANES)
            cum = _fox_cum(flog.reshape(batch, seq, LANES), b_pad, batch, seq)
            a = _fox_attention(qkv.reshape(batch, seq, -1),
                               cum.reshape(batch, N_HEADS, 1, seq), batch, seq)
        x2d = _tail(x2d, a.reshape(t, ATTN_WIDTH), p[i].reshape(t, PLE_DIM),
                    bf(wo), row(mn), bf(wu), bf(wd), row(pn), bf(wg), bf(wp), fg,
                    final=(i == len(layers) - 1))
    return x2d.reshape(batch, seq, d)
```

```python
import functools

import jax
import jax.numpy as jnp
import numpy as np
from jax import lax
from jax.experimental import pallas as pl
from jax.experimental.pallas import tpu as pltpu

D_MODEL = 1024
HEAD_DIM = 64
N_HEADS = 16
ATTN_WIDTH = N_HEADS * HEAD_DIM
SWA_KV_HEADS = 2
SWA_GROUP = N_HEADS // SWA_KV_HEADS
SWA_WINDOW = 128
ROPE_THETA = 500000.0
ROPE_DIM = 16
D_FF = 4 * D_MODEL
PLE_DIM = 256
Q_BLOCK = 128
RMS_EPS = 1e-6
NEG_INF = -1e30
SCALE = HEAD_DIM ** -0.5

LANES = 128
HEADS_PER_TILE = LANES // HEAD_DIM
N_HEAD_TILES = ATTN_WIDTH // LANES
TOKEN_TILE = 512
FF_CHUNK = 1024
VMEM_LIMIT = 56 * 1024 * 1024

SB_STATIC = 3
SB_EXIT = -104.0
FOX_Q = 512
FOX_K = 256

assert FOX_K == 2 * LANES and FOX_Q % FOX_K == 0

F32 = jnp.float32
BF16 = jnp.bfloat16


def _params(n_axes, parallel=True):
    sem = ("parallel" if parallel else "arbitrary",) * n_axes
    return pltpu.CompilerParams(dimension_semantics=sem, vmem_limit_bytes=VMEM_LIMIT)


def _const_spec(shape):
    nd = len(shape)
    return pl.BlockSpec(shape, lambda *_: (0,) * nd, pipeline_mode=pl.Buffered(1))


def _rms(x, g):
    ms = jnp.mean(x * x, axis=-1, keepdims=True)
    return (x * lax.rsqrt(ms + RMS_EPS)) * g


def _dot(a, b):
    return jnp.dot(a, b, preferred_element_type=F32)


def _dot_t(a, b):
    return lax.dot_general(a, b, (((1,), (1,)), ((), ())), preferred_element_type=F32)


def _head_masks(rows=Q_BLOCK):
    lane = lax.broadcasted_iota(jnp.int32, (rows, LANES), 1)
    m0 = jnp.where(lane < HEAD_DIM, 1.0, 0.0).astype(BF16)
    m1 = jnp.where(lane < HEAD_DIM, 0.0, 1.0).astype(BF16)
    return m0, m1


def _inproj_kernel(x_ref, g_ref, w_ref, o_ref, h_ref, *, n_out):
    h_ref[...] = _rms(x_ref[...], g_ref[...]).astype(BF16)
    for c in range(0, n_out, FF_CHUNK):
        o_ref[:, c:c + FF_CHUNK] = _dot(h_ref[...], w_ref[:, c:c + FF_CHUNK]).astype(BF16)


def _inproj(x2d, g, w):
    t, d = x2d.shape
    n_out = w.shape[1]
    return pl.pallas_call(
        functools.partial(_inproj_kernel, n_out=n_out),
        grid=(t // TOKEN_TILE,),
        in_specs=[pl.BlockSpec((TOKEN_TILE, d), lambda i: (i, 0)),
                  _const_spec((1, d)), _const_spec((d, n_out))],
        out_specs=pl.BlockSpec((TOKEN_TILE, n_out), lambda i: (i, 0)),
        out_shape=jax.ShapeDtypeStruct((t, n_out), BF16),
        scratch_shapes=[pltpu.VMEM((TOKEN_TILE, d), BF16)],
        compiler_params=_params(1),
        name="inproj",
    )(x2d, g, w)


def _inproj_fox_kernel(x_ref, g_ref, w_ref, wf_ref, o_ref, f_ref, h_ref, *, n_out):
    h_ref[...] = _rms(x_ref[...], g_ref[...]).astype(BF16)
    for c in range(0, n_out, FF_CHUNK):
        o_ref[:, c:c + FF_CHUNK] = _dot(h_ref[...], w_ref[:, c:c + FF_CHUNK]).astype(BF16)
    f_ref[...] = _dot(h_ref[...], wf_ref[...])


def _inproj_fox(x2d, g, w, wf):
    t, d = x2d.shape
    n_out = w.shape[1]
    return pl.pallas_call(
        functools.partial(_inproj_fox_kernel, n_out=n_out),
        grid=(t // TOKEN_TILE,),
        in_specs=[pl.BlockSpec((TOKEN_TILE, d), lambda i: (i, 0)),
                  _const_spec((1, d)), _const_spec((d, n_out)), _const_spec((d, LANES))],
        out_specs=[pl.BlockSpec((TOKEN_TILE, n_out), lambda i: (i, 0)),
                   pl.BlockSpec((TOKEN_TILE, LANES), lambda i: (i, 0))],
        out_shape=[jax.ShapeDtypeStruct((t, n_out), BF16),
                   jax.ShapeDtypeStruct((t, LANES), F32)],
        scratch_shapes=[pltpu.VMEM((TOKEN_TILE, d), BF16)],
        compiler_params=_params(1),
        name="inproj_fox",
    )(x2d, g, w, wf)


def _inproj_swa_kernel(x_ref, g_ref, w_ref, c_ref, s1_ref, s2_ref, o_ref, h_ref, *, n_rope, n_out):
    h_ref[...] = _rms(x_ref[...], g_ref[...]).astype(BF16)
    cos, s1, s2 = c_ref[...], s1_ref[...], s2_ref[...]
    for c in range(0, n_out, LANES):
        y = _dot(h_ref[...], w_ref[:, c:c + LANES])
        if c < n_rope:
            y = y * cos + pltpu.roll(y, 8, 1) * s1 + pltpu.roll(y, LANES - 8, 1) * s2
        o_ref[:, c:c + LANES] = y.astype(BF16)


def _inproj_swa(x2d, g, w, cos, s1, s2, seq):
    t, d = x2d.shape
    n_out = w.shape[1]
    n_rope = ATTN_WIDTH + SWA_KV_HEADS * HEAD_DIM
    tiles_per_seq = seq // TOKEN_TILE
    tab = pl.BlockSpec((TOKEN_TILE, LANES), lambda i: (i % tiles_per_seq, 0))
    return pl.pallas_call(
        functools.partial(_inproj_swa_kernel, n_rope=n_rope, n_out=n_out),
        grid=(t // TOKEN_TILE,),
        in_specs=[pl.BlockSpec((TOKEN_TILE, d), lambda i: (i, 0)),
                  _const_spec((1, d)), _const_spec((d, n_out)), tab, tab, tab],
        out_specs=pl.BlockSpec((TOKEN_TILE, n_out), lambda i: (i, 0)),
        out_shape=jax.ShapeDtypeStruct((t, n_out), BF16),
        scratch_shapes=[pltpu.VMEM((TOKEN_TILE, d), BF16)],
        compiler_params=_params(1),
        name="inproj_swa",
    )(x2d, g, w, cos, s1, s2)


def _rope_tables(seq):
    half = ROPE_DIM // 2
    inv_freq = ROPE_THETA ** (-jnp.arange(half, dtype=F32) / half)
    ang = jnp.arange(seq, dtype=jnp.int32).astype(F32)[:, None] * inv_freq[None, :]
    cos, sin = jnp.cos(ang), jnp.sin(ang)
    one = jnp.ones((seq, HEAD_DIM - ROPE_DIM), F32)
    zero = jnp.zeros((seq, HEAD_DIM - ROPE_DIM), F32)
    zh = jnp.zeros((seq, half), F32)
    c_head = jnp.concatenate([cos, cos, one], axis=1)
    s1_head = jnp.concatenate([zh, sin, zero], axis=1)
    s2_head = jnp.concatenate([-sin, zh, zero], axis=1)
    tile = lambda a: jnp.concatenate([a] * HEADS_PER_TILE, axis=1)
    return tile(c_head), tile(s1_head), tile(s2_head)


def _sb_kernel(q_ref, k_ref, v_ref, uu_ref, o_ref, acc_ref, r_ref, *, n_blocks):
    sel = _head_masks()
    row = lax.broadcasted_iota(jnp.int32, (Q_BLOCK, Q_BLOCK), 0)
    col = lax.broadcasted_iota(jnp.int32, (Q_BLOCK, Q_BLOCK), 1)
    strict = col < row

    def pair(qh, k_off, r_in, diagonal):
        kj = k_ref[pl.ds(k_off, Q_BLOCK), :]
        vj = v_ref[pl.ds(k_off, Q_BLOCK), :]
        ws, r_out = [], []
        for h in range(HEADS_PER_TILE):
            z = _dot_t(qh[h], kj)
            sp = jnp.maximum(z, 0.0) + jnp.log(1.0 + jnp.exp(-jnp.abs(z)))
            lom = -sp
            if diagonal:
                lom = jnp.where(strict, lom, 0.0)
            hi = lom.astype(BF16)
            lo = (lom - hi.astype(F32)).astype(BF16)
            cs = _dot(jnp.concatenate([hi, lo], axis=1), uu_ref[...])
            tail, tot = cs[:, :Q_BLOCK], cs[:, Q_BLOCK:]
            if r_in is not None:
                tail, tot = tail + r_in[h], tot + r_in[h]
            w = jnp.exp((z - sp) + tail)
            if diagonal:
                w = jnp.where(strict, w, 0.0)
            ws.append(w.astype(BF16))
            r_out.append(tot)
        vstack = jnp.concatenate([vj * sel[0], vj * sel[1]], axis=0)
        return _dot(jnp.concatenate(ws, axis=1), vstack), r_out

    def q_block(i, n_static):
        static_i = isinstance(i, int)
        q_off = i * Q_BLOCK if static_i else pl.multiple_of(i * Q_BLOCK, Q_BLOCK)
        qi = q_ref[pl.ds(q_off, Q_BLOCK), :] * jnp.asarray(SCALE, BF16)
        qh = (qi * sel[0], qi * sel[1])
        acc, r = pair(qh, q_off, None, True)
        for d in range(1, n_static):
            k_off = q_off - d * Q_BLOCK
            pv, r = pair(qh, k_off if static_i else pl.multiple_of(k_off, Q_BLOCK), r, False)
            acc = acc + pv
        if static_i and i < n_static:
            o_ref[pl.ds(q_off, Q_BLOCK), :] = acc.astype(BF16)
            return
        acc_ref[...] = acc
        for h in range(HEADS_PER_TILE):
            r_ref[h] = r[h]

        def more(c):
            j, r_max = c
            return (j >= 0) & (r_max >= SB_EXIT)

        def step(c):
            j, _ = c
            pv, r_new = pair(qh, pl.multiple_of(j * Q_BLOCK, Q_BLOCK),
                             [r_ref[h] for h in range(HEADS_PER_TILE)], False)
            acc_ref[...] += pv
            for h in range(HEADS_PER_TILE):
                r_ref[h] = r_new[h]
            return j - 1, jnp.max(jnp.maximum(r_new[0], r_new[1]))

        lax.while_loop(more, step, (i - n_static, jnp.max(jnp.maximum(r[0], r[1]))))
        o_ref[pl.ds(q_off, Q_BLOCK), :] = acc_ref[...].astype(BF16)

    for i in range(min(SB_STATIC - 1, n_blocks)):
        q_block(i, i + 1)

    def body(i, _):
        q_block(i, SB_STATIC)
        return 0

    lax.fori_loop(SB_STATIC - 1, n_blocks, body, 0)


def _cumsum_weights():
    a = np.arange(Q_BLOCK)
    u = (a[:, None] > a[None, :]).astype(np.float32)
    half = np.concatenate([u, np.ones_like(u)], axis=1)
    return jnp.asarray(np.concatenate([half, half], axis=0), BF16)


def _sb_attention(qkv, batch, seq):
    blk = lambda off: pl.BlockSpec((None, seq, LANES), lambda b, t: (b, 0, off + t))
    return pl.pallas_call(
        functools.partial(_sb_kernel, n_blocks=seq // Q_BLOCK),
        grid=(batch, N_HEAD_TILES),
        in_specs=[blk(0), blk(N_HEAD_TILES), blk(2 * N_HEAD_TILES),
                  _const_spec((2 * Q_BLOCK, 2 * Q_BLOCK))],
        out_specs=pl.BlockSpec((None, seq, LANES), lambda b, t: (b, 0, t)),
        out_shape=jax.ShapeDtypeStruct((batch, seq, ATTN_WIDTH), BF16),
        scratch_shapes=[pltpu.VMEM((Q_BLOCK, LANES), F32),
                        pltpu.VMEM((HEADS_PER_TILE, Q_BLOCK, Q_BLOCK), F32)],
        compiler_params=_params(2),
        name="sb_attention",
    )(qkv, qkv, qkv, _cumsum_weights())


def _fox_cum_kernel(f_ref, b_ref, tri_ref, o_ref, *, n_blocks):
    x = f_ref[...] + b_ref[...]
    lf = jnp.minimum(x, 0.0) - jnp.log1p(jnp.exp(-jnp.abs(x)))
    lft = lf.T
    carry = jnp.zeros((LANES, 1), F32)
    for c in range(n_blocks):
        blk = lft[:, c * Q_BLOCK:(c + 1) * Q_BLOCK]
        hi = blk.astype(BF16)
        r1 = blk - hi.astype(F32)
        mid = r1.astype(BF16)
        lo = (r1 - mid.astype(F32)).astype(BF16)
        cs = _dot(hi, tri_ref[...]) + _dot(mid, tri_ref[...]) + _dot(lo, tri_ref[...]) + carry
        o_ref[:, c * Q_BLOCK:(c + 1) * Q_BLOCK] = cs[:N_HEADS]
        carry = cs[:, Q_BLOCK - 1:Q_BLOCK]


def _fox_cum(flog, b_pad, batch, seq):
    a = np.arange(Q_BLOCK)
    tri = jnp.asarray((a[:, None] <= a[None, :]).astype(np.float32), BF16)
    return pl.pallas_call(
        functools.partial(_fox_cum_kernel, n_blocks=seq // Q_BLOCK),
        grid=(batch,),
        in_specs=[pl.BlockSpec((None, seq, LANES), lambda b: (b, 0, 0)),
                  _const_spec((1, LANES)), _const_spec((Q_BLOCK, Q_BLOCK))],
        out_specs=pl.BlockSpec((None, N_HEADS, seq), lambda b: (b, 0, 0)),
        out_shape=jax.ShapeDtypeStruct((batch, N_HEADS, seq), F32),
        compiler_params=_params(1),
        name="fox_cum",
    )(flog, b_pad, tri)


def _fox_kernel(q_ref, k_ref, v_ref, c_ref, o_ref, qh_ref, acc_ref, l_ref, m_ref, *, n_blocks):
    sel_q = _head_masks(FOX_Q)
    sel_k = _head_masks(FOX_K)

    def block(q_off, k_off, r0, masked):
        rows = slice(r0, FOX_Q)
        n_rows = FOX_Q - r0
        head0 = lax.broadcasted_iota(jnp.int32, (n_rows, LANES), 1) < HEAD_DIM
        kj = k_ref[pl.ds(k_off, FOX_K), :]
        vj = v_ref[pl.ds(k_off, FOX_K), :]
        ps, alphas = [], []
        for h in range(HEADS_PER_TILE):
            cj = c_ref[h, :, pl.ds(k_off, FOX_K)]
            s = _dot_t(qh_ref[h, rows], kj) - cj
            if masked:
                col_minus_row = (lax.broadcasted_iota(jnp.int32, (n_rows, FOX_K), 1)
                                 - lax.broadcasted_iota(jnp.int32, (n_rows, FOX_K), 0))
                s = jnp.where(col_minus_row <= q_off + r0 - k_off, s, NEG_INF)
            m_old = m_ref[h, rows]
            m_blk = jnp.max(jnp.maximum(s[:, :LANES], s[:, LANES:]), axis=-1, keepdims=True)
            m_new = jnp.maximum(m_old, jnp.broadcast_to(m_blk, (n_rows, LANES)))
            m_ref[h, rows] = m_new
            alphas.append(jnp.exp(m_old - m_new))
            ps.append(jnp.concatenate([jnp.exp(s[:, :LANES] - m_new),
                                       jnp.exp(s[:, LANES:] - m_new)], axis=1).astype(BF16))
        alpha = jnp.where(head0, alphas[0], alphas[1])
        vext = jnp.concatenate(
            [jnp.concatenate([vj * sel_k[0], sel_k[0]], axis=1),
             jnp.concatenate([vj * sel_k[1], sel_k[1]], axis=1)], axis=0)
        pv = _dot(jnp.concatenate(ps, axis=1), vext)
        acc_ref[rows] = acc_ref[rows] * alpha + pv[:, :LANES]
        l_ref[rows] = l_ref[rows] * alpha + pv[:, LANES:]

    def q_block(i, _):
        q_off = pl.multiple_of(i * FOX_Q, FOX_Q)
        qi = q_ref[pl.ds(q_off, FOX_Q), :] * jnp.asarray(SCALE, BF16)
        for h in range(HEADS_PER_TILE):
            qh_ref[h] = qi * sel_q[h]
        acc_ref[...] = jnp.zeros_like(acc_ref)
        l_ref[...] = jnp.zeros_like(l_ref)
        m_ref[...] = jnp.full_like(m_ref, NEG_INF)

        def below_diagonal(j, _):
            block(q_off, pl.multiple_of(j * FOX_K, FOX_K), 0, False)
            return 0

        lax.fori_loop(0, i * (FOX_Q // FOX_K), below_diagonal, 0)
        for d in range(FOX_Q // FOX_K):
            block(q_off, pl.multiple_of(q_off + d * FOX_K, FOX_K), d * FOX_K, True)
        o_ref[pl.ds(q_off, FOX_Q), :] = (acc_ref[...] / l_ref[...]).astype(BF16)
        return 0

    lax.fori_loop(0, n_blocks, q_block, 0)


def _fox_attention(qkv, cum, batch, seq):
    blk = lambda off: pl.BlockSpec((None, seq, LANES), lambda b, t: (b, 0, off + t))
    return pl.pallas_call(
        functools.partial(_fox_kernel, n_blocks=seq // FOX_Q),
        grid=(batch, N_HEAD_TILES),
        in_specs=[blk(0), blk(N_HEAD_TILES), blk(2 * N_HEAD_TILES),
                  pl.BlockSpec((None, HEADS_PER_TILE, 1, seq), lambda b, t: (b, t, 0, 0))],
        out_specs=pl.BlockSpec((None, seq, LANES), lambda b, t: (b, 0, t)),
        out_shape=jax.ShapeDtypeStruct((batch, seq, ATTN_WIDTH), BF16),
        scratch_shapes=[pltpu.VMEM((HEADS_PER_TILE, FOX_Q, LANES), BF16),
                        pltpu.VMEM((FOX_Q, LANES), F32),
                        pltpu.VMEM((FOX_Q, LANES), F32),
                        pltpu.VMEM((HEADS_PER_TILE, FOX_Q, LANES), F32)],
        compiler_params=_params(2),
        name="fox_attention",
    )(qkv, qkv, qkv, cum)


def _swa_kernel(sink_ref, q_ref, kp_ref, kc_ref, vp_ref, vc_ref, o_ref):
    g = pl.program_id(1)
    n = pl.program_id(2)
    lane = lax.broadcasted_iota(jnp.int32, (2 * Q_BLOCK, LANES), 1)
    kw = jnp.concatenate([kp_ref[...], kc_ref[...]], axis=0).astype(F32)
    vw = jnp.concatenate([vp_ref[...], vc_ref[...]], axis=0).astype(F32)
    mine = (lane >= g * HEAD_DIM) & (lane < (g + 1) * HEAD_DIM)
    kg = jnp.where(mine, kw, 0.0)
    vg = jnp.where(mine, vw, 0.0)
    k2 = (kg + pltpu.roll(kg, HEAD_DIM, 1)).astype(BF16)
    v2 = (vg + pltpu.roll(vg, HEAD_DIM, 1)).astype(BF16)

    a = lax.broadcasted_iota(jnp.int32, (Q_BLOCK, 2 * Q_BLOCK), 0)
    c = lax.broadcasted_iota(jnp.int32, (Q_BLOCK, 2 * Q_BLOCK), 1)
    mask = (c > a) & (c <= a + SWA_WINDOW) & (c + n * Q_BLOCK >= Q_BLOCK)
    qlane = lax.broadcasted_iota(jnp.int32, (Q_BLOCK, LANES), 1)
    q_h0 = qlane < HEAD_DIM
    sel = _head_masks()

    for t in range(SWA_GROUP // HEADS_PER_TILE):
        qt = q_ref[:, t * LANES:(t + 1) * LANES] * jnp.asarray(SCALE, BF16)
        outs = []
        for h in range(HEADS_PER_TILE):
            s = jnp.where(mask, _dot_t(qt * sel[h], k2), NEG_INF)
            sink = sink_ref[g * SWA_GROUP + t * HEADS_PER_TILE + h]
            m = jnp.maximum(jnp.max(s, axis=-1, keepdims=True), sink)
            e = jnp.exp(s - m)
            den = jnp.sum(e, axis=-1, keepdims=True) + jnp.exp(sink - m)
            outs.append(_dot((e / den).astype(BF16), v2))
        o_ref[:, t * LANES:(t + 1) * LANES] = jnp.where(q_h0, outs[0], outs[1]).astype(BF16)


def _swa_attention(qkv, sinks, batch, seq):
    nb = seq // Q_BLOCK
    gw = SWA_GROUP * HEAD_DIM
    k_tile = ATTN_WIDTH // LANES
    v_tile = k_tile + 1
    prev = lambda tile: pl.BlockSpec((None, Q_BLOCK, LANES),
                                     lambda b, g, n, s: (b, jnp.maximum(n - 1, 0), tile))
    cur = lambda tile: pl.BlockSpec((None, Q_BLOCK, LANES), lambda b, g, n, s: (b, n, tile))
    grid_spec = pltpu.PrefetchScalarGridSpec(
        num_scalar_prefetch=1,
        grid=(batch, SWA_KV_HEADS, nb),
        in_specs=[pl.BlockSpec((None, Q_BLOCK, gw), lambda b, g, n, s: (b, n, g)),
                  prev(k_tile), cur(k_tile), prev(v_tile), cur(v_tile)],
        out_specs=pl.BlockSpec((None, Q_BLOCK, gw), lambda b, g, n, s: (b, n, g)),
    )
    return pl.pallas_call(
        _swa_kernel,
        grid_spec=grid_spec,
        out_shape=jax.ShapeDtypeStruct((batch, seq, ATTN_WIDTH), BF16),
        compiler_params=_params(3),
        name="swa_attention",
    )(sinks, qkv, qkv, qkv, qkv, qkv)


def _tail_kernel(x_ref, a_ref, p_ref, wo_ref, mg_ref, wu_ref, wd_ref, pg_ref, wg_ref, wp_ref,
                 fg_ref, o_ref, h_ref, *, final):
    x1 = x_ref[...] + _dot(a_ref[...], wo_ref[...])
    h_ref[...] = _rms(x1, mg_ref[...]).astype(BF16)
    x2 = x1
    for c in range(0, D_FF, FF_CHUNK):
        u = jnp.maximum(_dot(h_ref[...], wu_ref[:, c:c + FF_CHUNK]), 0.0)
        x2 = x2 + _dot((u * u).astype(BF16), wd_ref[c:c + FF_CHUNK, :])
    h3 = _rms(x2, pg_ref[...]).astype(BF16)
    gate = 1.0 / (1.0 + jnp.exp(-_dot(h3, wg_ref[...])))
    x3 = x2 + _dot(p_ref[...].astype(BF16), wp_ref[...]) * gate
    o_ref[...] = _rms(x3, fg_ref[...]) if final else x3


def _tail(x2d, a2d, p2d, wo, mg, wu, wd, pg, wg, wp, fg, final):
    t, d = x2d.shape
    tok = lambda w: pl.BlockSpec((TOKEN_TILE, w), lambda i: (i, 0))
    return pl.pallas_call(
        functools.partial(_tail_kernel, final=final),
        grid=(t // TOKEN_TILE,),
        in_specs=[tok(d), tok(ATTN_WIDTH), tok(PLE_DIM),
                  _const_spec((ATTN_WIDTH, d)), _const_spec((1, d)),
                  _const_spec((d, D_FF)), _const_spec((D_FF, d)), _const_spec((1, d)),
                  _const_spec((d, d)), _const_spec((PLE_DIM, d)), _const_spec((1, d))],
        out_specs=tok(d),
        out_shape=jax.ShapeDtypeStruct((t, d), F32),
        scratch_shapes=[pltpu.VMEM((TOKEN_TILE, d), BF16)],
        compiler_params=_params(1),
        name="layer_tail",
    )(x2d, a2d, p2d, wo, mg, wu, wd, pg, wg, wp, fg)


def kernel(x, p, attn_norm_0, w_in_0, w_out_0, mlp_norm_0, w_up_0, w_down_0, ple_norm_0, w_ple_gate_0, w_ple_proj_0, attn_norm_1, w_in_1, w_out_1, sinks_1, mlp_norm_1, w_up_1, w_down_1, ple_norm_1, w_ple_gate_1, w_ple_proj_1, attn_norm_2, w_in_2, w_out_2, b_forget_2, mlp_norm_2, w_up_2, w_down_2, ple_norm_2, w_ple_gate_2, w_ple_proj_2, attn_norm_3, w_in_3, w_out_3, mlp_norm_3, w_up_3, w_down_3, ple_norm_3, w_ple_gate_3, w_ple_proj_3, final_norm):
    batch, seq, d = x.shape
    t = batch * seq
    layers = [
        (attn_norm_0, w_in_0, w_out_0, None, mlp_norm_0, w_up_0, w_down_0, ple_norm_0, w_ple_gate_0, w_ple_proj_0),
        (attn_norm_1, w_in_1, w_out_1, sinks_1, mlp_norm_1, w_up_1, w_down_1, ple_norm_1, w_ple_gate_1, w_ple_proj_1),
        (attn_norm_2, w_in_2, w_out_2, b_forget_2, mlp_norm_2, w_up_2, w_down_2, ple_norm_2, w_ple_gate_2, w_ple_proj_2),
        (attn_norm_3, w_in_3, w_out_3, None, mlp_norm_3, w_up_3, w_down_3, ple_norm_3, w_ple_gate_3, w_ple_proj_3),
    ]
    row = lambda g: g.reshape(1, d)
    bf = lambda w: w.astype(BF16)
    x2d = x.reshape(t, d)
    fg = row(final_norm)
    for i, (an, wi, wo, extra, mn, wu, wd, pn, wg, wp) in enumerate(layers):
        kind = i % 3
        if kind == 0:
            qkv = _inproj(x2d, row(an), bf(wi))
            a = _sb_attention(qkv.reshape(batch, seq, -1), batch, seq)
        elif kind == 1:
            cos, s1, s2 = _rope_tables(seq)
            qkv = _inproj_swa(x2d, row(an), bf(wi), cos, s1, s2, seq)
            a = _swa_attention(qkv.reshape(batch, seq, -1), extra.astype(F32), batch, seq)
        else:
            wf = jnp.pad(wi[:, 3 * ATTN_WIDTH:], ((0, 0), (0, LANES - N_HEADS)))
            qkv, flog = _inproj_fox(x2d, row(an), bf(wi[:, :3 * ATTN_WIDTH]), bf(wf))
            b_pad = jnp.pad(extra.astype(F32), (0, LANES - N_HEADS)).reshape(1, LANES)
            cum = _fox_cum(flog.reshape(batch, seq, LANES), b_pad, batch, seq)
            a = _fox_attention(qkv.reshape(batch, seq, -1),
                               cum.reshape(batch, N_HEADS, 1, seq), batch, seq)
        x2d = _tail(x2d, a.reshape(t, ATTN_WIDTH), p[i].reshape(t, PLE_DIM),
                    bf(wo), row(mn), bf(wu), bf(wd), row(pn), bf(wg), bf(wp), fg,
                    final=(i == len(layers) - 1))
    return x2d.reshape(batch, seq, d)
```

```python
import functools

import jax
import jax.numpy as jnp
import numpy as np
from jax import lax
from jax.experimental import pallas as pl
from jax.experimental.pallas import tpu as pltpu

D_MODEL = 1024
HEAD_DIM = 64
N_HEADS = 16
ATTN_WIDTH = N_HEADS * HEAD_DIM
SWA_KV_HEADS = 2
SWA_GROUP = N_HEADS // SWA_KV_HEADS
SWA_WINDOW = 128
ROPE_THETA = 500000.0
ROPE_DIM = 16
D_FF = 4 * D_MODEL
PLE_DIM = 256
Q_BLOCK = 128
RMS_EPS = 1e-6
NEG_INF = -1e30
SCALE = HEAD_DIM ** -0.5

LANES = 128
HEADS_PER_TILE = LANES // HEAD_DIM
N_HEAD_TILES = ATTN_WIDTH // LANES
TOKEN_TILE = 512
FF_CHUNK = 1024
VMEM_LIMIT = 56 * 1024 * 1024

SB_Q = 2 * Q_BLOCK
SB_EXTRA = 2
SB_EXIT = 104.0
LOG2E = 1.4426950408889634
FOX_Q = 512
FOX_K = 256

assert FOX_K == 2 * LANES and FOX_Q % FOX_K == 0
assert SB_Q == 2 * Q_BLOCK and SB_EXTRA <= SB_Q // Q_BLOCK

F32 = jnp.float32
BF16 = jnp.bfloat16


def _params(n_axes):
    return pltpu.CompilerParams(dimension_semantics=("parallel",) * n_axes,
                                vmem_limit_bytes=VMEM_LIMIT)


def _const_spec(shape):
    nd = len(shape)
    return pl.BlockSpec(shape, lambda *_: (0,) * nd, pipeline_mode=pl.Buffered(1))


def _rms(x, g):
    ms = jnp.mean(x * x, axis=-1, keepdims=True)
    return (x * lax.rsqrt(ms + RMS_EPS)) * g


def _dot(a, b):
    return jnp.dot(a, b, preferred_element_type=F32)


def _dot_t(a, b):
    return lax.dot_general(a, b, (((1,), (1,)), ((), ())), preferred_element_type=F32)


def _head_masks(rows=Q_BLOCK):
    lane = lax.broadcasted_iota(jnp.int32, (rows, LANES), 1)
    m0 = jnp.where(lane < HEAD_DIM, 1.0, 0.0).astype(BF16)
    m1 = jnp.where(lane < HEAD_DIM, 0.0, 1.0).astype(BF16)
    return m0, m1


def _inproj_kernel(x_ref, g_ref, w_ref, o_ref, h_ref, *, n_out):
    h_ref[...] = _rms(x_ref[...], g_ref[...]).astype(BF16)
    for c in range(0, n_out, FF_CHUNK):
        o_ref[:, c:c + FF_CHUNK] = _dot(h_ref[...], w_ref[:, c:c + FF_CHUNK]).astype(BF16)


def _inproj(x2d, g, w):
    t, d = x2d.shape
    n_out = w.shape[1]
    return pl.pallas_call(
        functools.partial(_inproj_kernel, n_out=n_out),
        grid=(t // TOKEN_TILE,),
        in_specs=[pl.BlockSpec((TOKEN_TILE, d), lambda i: (i, 0)),
                  _const_spec((1, d)), _const_spec((d, n_out))],
        out_specs=pl.BlockSpec((TOKEN_TILE, n_out), lambda i: (i, 0)),
        out_shape=jax.ShapeDtypeStruct((t, n_out), BF16),
        scratch_shapes=[pltpu.VMEM((TOKEN_TILE, d), BF16)],
        compiler_params=_params(1),
        name="inproj",
    )(x2d, g, w)


def _inproj_fox_kernel(x_ref, g_ref, w_ref, wf_ref, o_ref, f_ref, h_ref, *, n_out):
    h_ref[...] = _rms(x_ref[...], g_ref[...]).astype(BF16)
    for c in range(0, n_out, FF_CHUNK):
        o_ref[:, c:c + FF_CHUNK] = _dot(h_ref[...], w_ref[:, c:c + FF_CHUNK]).astype(BF16)
    f_ref[...] = _dot(h_ref[...], wf_ref[...])


def _inproj_fox(x2d, g, w, wf):
    t, d = x2d.shape
    n_out = w.shape[1]
    return pl.pallas_call(
        functools.partial(_inproj_fox_kernel, n_out=n_out),
        grid=(t // TOKEN_TILE,),
        in_specs=[pl.BlockSpec((TOKEN_TILE, d), lambda i: (i, 0)),
                  _const_spec((1, d)), _const_spec((d, n_out)), _const_spec((d, LANES))],
        out_specs=[pl.BlockSpec((TOKEN_TILE, n_out), lambda i: (i, 0)),
                   pl.BlockSpec((TOKEN_TILE, LANES), lambda i: (i, 0))],
        out_shape=[jax.ShapeDtypeStruct((t, n_out), BF16),
                   jax.ShapeDtypeStruct((t, LANES), F32)],
        scratch_shapes=[pltpu.VMEM((TOKEN_TILE, d), BF16)],
        compiler_params=_params(1),
        name="inproj_fox",
    )(x2d, g, w, wf)


def _inproj_swa_kernel(x_ref, g_ref, w_ref, c_ref, s1_ref, s2_ref, o_ref, h_ref, *, n_rope, n_out):
    h_ref[...] = _rms(x_ref[...], g_ref[...]).astype(BF16)
    cos, s1, s2 = c_ref[...], s1_ref[...], s2_ref[...]
    for c in range(0, n_out, LANES):
        y = _dot(h_ref[...], w_ref[:, c:c + LANES])
        if c < n_rope:
            y = y * cos + pltpu.roll(y, 8, 1) * s1 + pltpu.roll(y, LANES - 8, 1) * s2
        o_ref[:, c:c + LANES] = y.astype(BF16)


def _inproj_swa(x2d, g, w, cos, s1, s2, seq):
    t, d = x2d.shape
    n_out = w.shape[1]
    n_rope = ATTN_WIDTH + SWA_KV_HEADS * HEAD_DIM
    tiles_per_seq = seq // TOKEN_TILE
    tab = pl.BlockSpec((TOKEN_TILE, LANES), lambda i: (i % tiles_per_seq, 0))
    return pl.pallas_call(
        functools.partial(_inproj_swa_kernel, n_rope=n_rope, n_out=n_out),
        grid=(t // TOKEN_TILE,),
        in_specs=[pl.BlockSpec((TOKEN_TILE, d), lambda i: (i, 0)),
                  _const_spec((1, d)), _const_spec((d, n_out)), tab, tab, tab],
        out_specs=pl.BlockSpec((TOKEN_TILE, n_out), lambda i: (i, 0)),
        out_shape=jax.ShapeDtypeStruct((t, n_out), BF16),
        scratch_shapes=[pltpu.VMEM((TOKEN_TILE, d), BF16)],
        compiler_params=_params(1),
        name="inproj_swa",
    )(x2d, g, w, cos, s1, s2)


def _rope_tables(seq):
    half = ROPE_DIM // 2
    inv_freq = ROPE_THETA ** (-jnp.arange(half, dtype=F32) / half)
    ang = jnp.arange(seq, dtype=jnp.int32).astype(F32)[:, None] * inv_freq[None, :]
    cos, sin = jnp.cos(ang), jnp.sin(ang)
    one = jnp.ones((seq, HEAD_DIM - ROPE_DIM), F32)
    zero = jnp.zeros((seq, HEAD_DIM - ROPE_DIM), F32)
    zh = jnp.zeros((seq, half), F32)
    c_head = jnp.concatenate([cos, cos, one], axis=1)
    s1_head = jnp.concatenate([zh, sin, zero], axis=1)
    s2_head = jnp.concatenate([-sin, zh, zero], axis=1)
    tile = lambda a: jnp.concatenate([a] * HEADS_PER_TILE, axis=1)
    return tile(c_head), tile(s1_head), tile(s2_head)


def _sb_kernel(q_ref, k_ref, v_ref, uu_ref, o_ref, ks_ref, vs_ref, acc_ref, r_ref, *, n_blocks):
    sel = _head_masks()
    blocks_per_q = SB_Q // Q_BLOCK

    def stack_heads(j, _):
        rows = pl.ds(pl.multiple_of(j * Q_BLOCK, Q_BLOCK), Q_BLOCK)
        kj, vj = k_ref[rows, :], v_ref[rows, :]
        ks_ref[j] = jnp.concatenate([kj * sel[0], kj * sel[1]], axis=0)
        vs_ref[j] = jnp.concatenate([vj * sel[0], vj * sel[1]], axis=0)
        return 0

    lax.fori_loop(0, n_blocks * blocks_per_q, stack_heads, 0)

    def strict(n_rows):
        return (lax.broadcasted_iota(jnp.int32, (n_rows, Q_BLOCK), 1)
                < lax.broadcasted_iota(jnp.int32, (n_rows, Q_BLOCK), 0))

    def key_block(q, j, r_in, keep):
        z2 = _dot_t(q, ks_ref[j])
        ws, r_out = [], []
        for h in range(HEADS_PER_TILE):
            z = z2[:, h * Q_BLOCK:(h + 1) * Q_BLOCK]
            sp = jnp.maximum(z, 0.0) + jnp.log(1.0 + jnp.exp2(jnp.abs(z) * -LOG2E))
            spm = sp if keep is None else jnp.where(keep, sp, 0.0)
            hi = spm.astype(BF16)
            lo = (spm - hi.astype(F32)).astype(BF16)
            cs = _dot(jnp.concatenate([hi, lo], axis=1), uu_ref[...])
            tail, tot = cs[:, :Q_BLOCK], cs[:, Q_BLOCK:]
            if r_in is not None:
                tail, tot = tail + r_in[h], tot + r_in[h]
            w = jnp.exp2((z - tail) * LOG2E)
            if keep is not None:
                w = jnp.where(keep, w, 0.0)
            ws.append(w.astype(BF16))
            r_out.append(tot)
        return _dot(jnp.concatenate(ws, axis=1), vs_ref[j]), r_out

    def q_block(i, n_extra):
        static_i = isinstance(i, int)
        q_off = i * SB_Q if static_i else pl.multiple_of(i * SB_Q, SB_Q)
        j_diag = i * blocks_per_q
        scale = jnp.asarray(SCALE, BF16)
        q = q_ref[pl.ds(q_off, SB_Q), :] * scale
        zeros = jnp.zeros((Q_BLOCK, LANES), F32)
        pv_lo, r_lo = key_block(q_ref[pl.ds(q_off + Q_BLOCK, Q_BLOCK), :] * scale,
                                j_diag + 1, None, strict(Q_BLOCK))
        r = [jnp.concatenate([zeros, r_lo[h]], axis=0) for h in range(HEADS_PER_TILE)]
        pv, r = key_block(q, j_diag, r, strict(SB_Q))
        acc = pv + jnp.concatenate([zeros, pv_lo], axis=0)
        for d in range(1, n_extra + 1):
            pv, r = key_block(q, j_diag - d, r, None)
            acc = acc + pv
        if static_i and j_diag <= n_extra:
            o_ref[pl.ds(q_off, SB_Q), :] = acc.astype(BF16)
            return
        acc_ref[...] = acc
        for h in range(HEADS_PER_TILE):
            r_ref[h] = r[h]

        def more(c):
            j, r_min = c
            return (j >= 0) & (r_min <= SB_EXIT)

        def step(c):
            j, _ = c
            pv, r_new = key_block(q, j, [r_ref[h] for h in range(HEADS_PER_TILE)], None)
            acc_ref[...] += pv
            for h in range(HEADS_PER_TILE):
                r_ref[h] = r_new[h]
            return j - 1, jnp.min(jnp.minimum(r_new[0], r_new[1]))

        lax.while_loop(more, step, (j_diag - n_extra - 1, jnp.min(jnp.minimum(r[0], r[1]))))
        o_ref[pl.ds(q_off, SB_Q), :] = acc_ref[...].astype(BF16)

    q_block(0, 0)

    def body(i, _):
        q_block(i, SB_EXTRA)
        return 0

    lax.fori_loop(1, n_blocks, body, 0)


def _cumsum_weights():
    a = np.arange(Q_BLOCK)
    u = (a[:, None] >= a[None, :]).astype(np.float32)
    half = np.concatenate([u, np.ones_like(u)], axis=1)
    return jnp.asarray(np.concatenate([half, half], axis=0), BF16)


def _sb_attention(qkv, batch, seq):
    blk = lambda off: pl.BlockSpec((None, seq, LANES), lambda b, t: (b, 0, off + t))
    return pl.pallas_call(
        functools.partial(_sb_kernel, n_blocks=seq // SB_Q),
        grid=(batch, N_HEAD_TILES),
        in_specs=[blk(0), blk(N_HEAD_TILES), blk(2 * N_HEAD_TILES),
                  _const_spec((2 * Q_BLOCK, 2 * Q_BLOCK))],
        out_specs=pl.BlockSpec((None, seq, LANES), lambda b, t: (b, 0, t)),
        out_shape=jax.ShapeDtypeStruct((batch, seq, ATTN_WIDTH), BF16),
        scratch_shapes=[pltpu.VMEM((seq // Q_BLOCK, 2 * Q_BLOCK, LANES), BF16),
                        pltpu.VMEM((seq // Q_BLOCK, 2 * Q_BLOCK, LANES), BF16),
                        pltpu.VMEM((SB_Q, LANES), F32),
                        pltpu.VMEM((HEADS_PER_TILE, SB_Q, Q_BLOCK), F32)],
        compiler_params=_params(2),
        name="sb_attention",
    )(qkv, qkv, qkv, _cumsum_weights())


def _fox_cum_kernel(f_ref, b_ref, tri_ref, o_ref, *, n_blocks):
    x = f_ref[...] + b_ref[...]
    lf = jnp.minimum(x, 0.0) - jnp.log1p(jnp.exp(-jnp.abs(x)))
    lft = lf.T
    carry = jnp.zeros((LANES, 1), F32)
    for c in range(n_blocks):
        blk = lft[:, c * Q_BLOCK:(c + 1) * Q_BLOCK]
        hi = blk.astype(BF16)
        r1 = blk - hi.astype(F32)
        mid = r1.astype(BF16)
        lo = (r1 - mid.astype(F32)).astype(BF16)
        cs = _dot(hi, tri_ref[...]) + _dot(mid, tri_ref[...]) + _dot(lo, tri_ref[...]) + carry
        o_ref[:, c * Q_BLOCK:(c + 1) * Q_BLOCK] = cs[:N_HEADS]
        carry = cs[:, Q_BLOCK - 1:Q_BLOCK]


def _fox_cum(flog, b_pad, batch, seq):
    a = np.arange(Q_BLOCK)
    tri = jnp.asarray((a[:, None] <= a[None, :]).astype(np.float32), BF16)
    return pl.pallas_call(
        functools.partial(_fox_cum_kernel, n_blocks=seq // Q_BLOCK),
        grid=(batch,),
        in_specs=[pl.BlockSpec((None, seq, LANES), lambda b: (b, 0, 0)),
                  _const_spec((1, LANES)), _const_spec((Q_BLOCK, Q_BLOCK))],
        out_specs=pl.BlockSpec((None, N_HEADS, seq), lambda b: (b, 0, 0)),
        out_shape=jax.ShapeDtypeStruct((batch, N_HEADS, seq), F32),
        compiler_params=_params(1),
        name="fox_cum",
    )(flog, b_pad, tri)


def _fox_kernel(q_ref, k_ref, v_ref, c_ref, o_ref, qh_ref, acc_ref, l_ref, m_ref, *, n_blocks):
    sel_q = _head_masks(FOX_Q)
    sel_k = _head_masks(FOX_K)

    def block(q_off, k_off, r0, masked):
        rows = slice(r0, FOX_Q)
        n_rows = FOX_Q - r0
        head0 = lax.broadcasted_iota(jnp.int32, (n_rows, LANES), 1) < HEAD_DIM
        kj = k_ref[pl.ds(k_off, FOX_K), :]
        vj = v_ref[pl.ds(k_off, FOX_K), :]
        ps, alphas = [], []
        for h in range(HEADS_PER_TILE):
            cj = c_ref[h, :, pl.ds(k_off, FOX_K)]
            s = _dot_t(qh_ref[h, rows], kj) - cj
            if masked:
                col_minus_row = (lax.broadcasted_iota(jnp.int32, (n_rows, FOX_K), 1)
                                 - lax.broadcasted_iota(jnp.int32, (n_rows, FOX_K), 0))
                s = jnp.where(col_minus_row <= q_off + r0 - k_off, s, NEG_INF)
            m_old = m_ref[h, rows]
            m_blk = jnp.max(jnp.maximum(s[:, :LANES], s[:, LANES:]), axis=-1, keepdims=True)
            m_new = jnp.maximum(m_old, jnp.broadcast_to(m_blk, (n_rows, LANES)))
            m_ref[h, rows] = m_new
            alphas.append(jnp.exp(m_old - m_new))
            ps.append(jnp.concatenate([jnp.exp(s[:, :LANES] - m_new),
                                       jnp.exp(s[:, LANES:] - m_new)], axis=1).astype(BF16))
        alpha = jnp.where(head0, alphas[0], alphas[1])
        vext = jnp.concatenate(
            [jnp.concatenate([vj * sel_k[0], sel_k[0]], axis=1),
             jnp.concatenate([vj * sel_k[1], sel_k[1]], axis=1)], axis=0)
        pv = _dot(jnp.concatenate(ps, axis=1), vext)
        acc_ref[rows] = acc_ref[rows] * alpha + pv[:, :LANES]
        l_ref[rows] = l_ref[rows] * alpha + pv[:, LANES:]

    def q_block(i, _):
        q_off = pl.multiple_of(i * FOX_Q, FOX_Q)
        qi = q_ref[pl.ds(q_off, FOX_Q), :] * jnp.asarray(SCALE, BF16)
        for h in range(HEADS_PER_TILE):
            qh_ref[h] = qi * sel_q[h]
        acc_ref[...] = jnp.zeros_like(acc_ref)
        l_ref[...] = jnp.zeros_like(l_ref)
        m_ref[...] = jnp.full_like(m_ref, NEG_INF)

        def below_diagonal(j, _):
            block(q_off, pl.multiple_of(j * FOX_K, FOX_K), 0, False)
            return 0

        lax.fori_loop(0, i * (FOX_Q // FOX_K), below_diagonal, 0)
        for d in range(FOX_Q // FOX_K):
            block(q_off, pl.multiple_of(q_off + d * FOX_K, FOX_K), d * FOX_K, True)
        o_ref[pl.ds(q_off, FOX_Q), :] = (acc_ref[...] / l_ref[...]).astype(BF16)
        return 0

    lax.fori_loop(0, n_blocks, q_block, 0)


def _fox_attention(qkv, cum, batch, seq):
    blk = lambda off: pl.BlockSpec((None, seq, LANES), lambda b, t: (b, 0, off + t))
    return pl.pallas_call(
        functools.partial(_fox_kernel, n_blocks=seq // FOX_Q),
        grid=(batch, N_HEAD_TILES),
        in_specs=[blk(0), blk(N_HEAD_TILES), blk(2 * N_HEAD_TILES),
                  pl.BlockSpec((None, HEADS_PER_TILE, 1, seq), lambda b, t: (b, t, 0, 0))],
        out_specs=pl.BlockSpec((None, seq, LANES), lambda b, t: (b, 0, t)),
        out_shape=jax.ShapeDtypeStruct((batch, seq, ATTN_WIDTH), BF16),
        scratch_shapes=[pltpu.VMEM((HEADS_PER_TILE, FOX_Q, LANES), BF16),
                        pltpu.VMEM((FOX_Q, LANES), F32),
                        pltpu.VMEM((FOX_Q, LANES), F32),
                        pltpu.VMEM((HEADS_PER_TILE, FOX_Q, LANES), F32)],
        compiler_params=_params(2),
        name="fox_attention",
    )(qkv, qkv, qkv, cum)


def _swa_kernel(sink_ref, q_ref, kp_ref, kc_ref, vp_ref, vc_ref, o_ref):
    g = pl.program_id(1)
    n = pl.program_id(2)
    lane = lax.broadcasted_iota(jnp.int32, (2 * Q_BLOCK, LANES), 1)
    kw = jnp.concatenate([kp_ref[...], kc_ref[...]], axis=0).astype(F32)
    vw = jnp.concatenate([vp_ref[...], vc_ref[...]], axis=0).astype(F32)
    mine = (lane >= g * HEAD_DIM) & (lane < (g + 1) * HEAD_DIM)
    kg = jnp.where(mine, kw, 0.0)
    vg = jnp.where(mine, vw, 0.0)
    k2 = (kg + pltpu.roll(kg, HEAD_DIM, 1)).astype(BF16)
    v2 = (vg + pltpu.roll(vg, HEAD_DIM, 1)).astype(BF16)

    a = lax.broadcasted_iota(jnp.int32, (Q_BLOCK, 2 * Q_BLOCK), 0)
    c = lax.broadcasted_iota(jnp.int32, (Q_BLOCK, 2 * Q_BLOCK), 1)
    mask = (c > a) & (c <= a + SWA_WINDOW) & (c + n * Q_BLOCK >= Q_BLOCK)
    qlane = lax.broadcasted_iota(jnp.int32, (Q_BLOCK, LANES), 1)
    q_h0 = qlane < HEAD_DIM
    sel = _head_masks()

    for t in range(SWA_GROUP // HEADS_PER_TILE):
        qt = q_ref[:, t * LANES:(t + 1) * LANES] * jnp.asarray(SCALE, BF16)
        outs = []
        for h in range(HEADS_PER_TILE):
            s = jnp.where(mask, _dot_t(qt * sel[h], k2), NEG_INF)
            sink = sink_ref[g * SWA_GROUP + t * HEADS_PER_TILE + h]
            m = jnp.maximum(jnp.max(s, axis=-1, keepdims=True), sink)
            e = jnp.exp(s - m)
            den = jnp.sum(e, axis=-1, keepdims=True) + jnp.exp(sink - m)
            outs.append(_dot((e / den).astype(BF16), v2))
        o_ref[:, t * LANES:(t + 1) * LANES] = jnp.where(q_h0, outs[0], outs[1]).astype(BF16)


def _swa_attention(qkv, sinks, batch, seq):
    nb = seq // Q_BLOCK
    gw = SWA_GROUP * HEAD_DIM
    k_tile = ATTN_WIDTH // LANES
    v_tile = k_tile + 1
    prev = lambda tile: pl.BlockSpec((None, Q_BLOCK, LANES),
                                     lambda b, g, n, s: (b, jnp.maximum(n - 1, 0), tile))
    cur = lambda tile: pl.BlockSpec((None, Q_BLOCK, LANES), lambda b, g, n, s: (b, n, tile))
    grid_spec = pltpu.PrefetchScalarGridSpec(
        num_scalar_prefetch=1,
        grid=(batch, SWA_KV_HEADS, nb),
        in_specs=[pl.BlockSpec((None, Q_BLOCK, gw), lambda b, g, n, s: (b, n, g)),
                  prev(k_tile), cur(k_tile), prev(v_tile), cur(v_tile)],
        out_specs=pl.BlockSpec((None, Q_BLOCK, gw), lambda b, g, n, s: (b, n, g)),
    )
    return pl.pallas_call(
        _swa_kernel,
        grid_spec=grid_spec,
        out_shape=jax.ShapeDtypeStruct((batch, seq, ATTN_WIDTH), BF16),
        compiler_params=_params(3),
        name="swa_attention",
    )(sinks, qkv, qkv, qkv, qkv, qkv)


def _tail_kernel(x_ref, a_ref, p_ref, wo_ref, mg_ref, wu_ref, wd_ref, pg_ref, wg_ref, wp_ref,
                 fg_ref, o_ref, h_ref, *, final):
    x1 = x_ref[...] + _dot(a_ref[...], wo_ref[...])
    h_ref[...] = _rms(x1, mg_ref[...]).astype(BF16)
    x2 = x1
    for c in range(0, D_FF, FF_CHUNK):
        u = jnp.maximum(_dot(h_ref[...], wu_ref[:, c:c + FF_CHUNK]), 0.0)
        x2 = x2 + _dot((u * u).astype(BF16), wd_ref[c:c + FF_CHUNK, :])
    h3 = _rms(x2, pg_ref[...]).astype(BF16)
    gate = 1.0 / (1.0 + jnp.exp(-_dot(h3, wg_ref[...])))
    x3 = x2 + _dot(p_ref[...].astype(BF16), wp_ref[...]) * gate
    o_ref[...] = _rms(x3, fg_ref[...]) if final else x3


def _tail(x2d, a2d, p2d, wo, mg, wu, wd, pg, wg, wp, fg, final):
    t, d = x2d.shape
    tok = lambda w: pl.BlockSpec((TOKEN_TILE, w), lambda i: (i, 0))
    return pl.pallas_call(
        functools.partial(_tail_kernel, final=final),
        grid=(t // TOKEN_TILE,),
        in_specs=[tok(d), tok(ATTN_WIDTH), tok(PLE_DIM),
                  _const_spec((ATTN_WIDTH, d)), _const_spec((1, d)),
                  _const_spec((d, D_FF)), _const_spec((D_FF, d)), _const_spec((1, d)),
                  _const_spec((d, d)), _const_spec((PLE_DIM, d)), _const_spec((1, d))],
        out_specs=tok(d),
        out_shape=jax.ShapeDtypeStruct((t, d), F32),
        scratch_shapes=[pltpu.VMEM((TOKEN_TILE, d), BF16)],
        compiler_params=_params(1),
        name="layer_tail",
    )(x2d, a2d, p2d, wo, mg, wu, wd, pg, wg, wp, fg)


def kernel(x, p, attn_norm_0, w_in_0, w_out_0, mlp_norm_0, w_up_0, w_down_0, ple_norm_0, w_ple_gate_0, w_ple_proj_0, attn_norm_1, w_in_1, w_out_1, sinks_1, mlp_norm_1, w_up_1, w_down_1, ple_norm_1, w_ple_gate_1, w_ple_proj_1, attn_norm_2, w_in_2, w_out_2, b_forget_2, mlp_norm_2, w_up_2, w_down_2, ple_norm_2, w_ple_gate_2, w_ple_proj_2, attn_norm_3, w_in_3, w_out_3, mlp_norm_3, w_up_3, w_down_3, ple_norm_3, w_ple_gate_3, w_ple_proj_3, final_norm):
    batch, seq, d = x.shape
    t = batch * seq
    layers = [
        (attn_norm_0, w_in_0, w_out_0, None, mlp_norm_0, w_up_0, w_down_0, ple_norm_0, w_ple_gate_0, w_ple_proj_0),
        (attn_norm_1, w_in_1, w_out_1, sinks_1, mlp_norm_1, w_up_1, w_down_1, ple_norm_1, w_ple_gate_1, w_ple_proj_1),
        (attn_norm_2, w_in_2, w_out_2, b_forget_2, mlp_norm_2, w_up_2, w_down_2, ple_norm_2, w_ple_gate_2, w_ple_proj_2),
        (attn_norm_3, w_in_3, w_out_3, None, mlp_norm_3, w_up_3, w_down_3, ple_norm_3, w_ple_gate_3, w_ple_proj_3),
    ]
    row = lambda g: g.reshape(1, d)
    bf = lambda w: w.astype(BF16)
    x2d = x.reshape(t, d)
    fg = row(final_norm)
    for i, (an, wi, wo, extra, mn, wu, wd, pn, wg, wp) in enumerate(layers):
        kind = i % 3
        if kind == 0:
            qkv = _inproj(x2d, row(an), bf(wi))
            a = _sb_attention(qkv.reshape(batch, seq, -1), batch, seq)
        elif kind == 1:
            cos, s1, s2 = _rope_tables(seq)
            qkv = _inproj_swa(x2d, row(an), bf(wi), cos, s1, s2, seq)
            a = _swa_attention(qkv.reshape(batch, seq, -1), extra.astype(F32), batch, seq)
        else:
            wf = jnp.pad(wi[:, 3 * ATTN_WIDTH:], ((0, 0), (0, LANES - N_HEADS)))
            qkv, flog = _inproj_fox(x2d, row(an), bf(wi[:, :3 * ATTN_WIDTH]), bf(wf))
            b_pad = jnp.pad(extra.astype(F32), (0, LANES - N_HEADS)).reshape(1, LANES)
            cum = _fox_cum(flog.reshape(batch, seq, LANES), b_pad, batch, seq)
            a = _fox_attention(qkv.reshape(batch, seq, -1),
                               cum.reshape(batch, N_HEADS, 1, seq), batch, seq)
        x2d = _tail(x2d, a.reshape(t, ATTN_WIDTH), p[i].reshape(t, PLE_DIM),
                    bf(wo), row(mn), bf(wu), bf(wd), row(pn), bf(wg), bf(wp), fg,
                    final=(i == len(layers) - 1))
    return x2d.reshape(batch, seq, d)
```

```python
import functools

import jax
import jax.numpy as jnp
import numpy as np
from jax import lax
from jax.experimental import pallas as pl
from jax.experimental.pallas import tpu as pltpu

D_MODEL = 1024
HEAD_DIM = 64
N_HEADS = 16
ATTN_WIDTH = N_HEADS * HEAD_DIM
SWA_KV_HEADS = 2
SWA_GROUP = N_HEADS // SWA_KV_HEADS
SWA_WINDOW = 128
ROPE_THETA = 500000.0
ROPE_DIM = 16
D_FF = 4 * D_MODEL
PLE_DIM = 256
Q_BLOCK = 128
RMS_EPS = 1e-6
NEG_INF = -1e30
SCALE = HEAD_DIM ** -0.5

LANES = 128
HEADS_PER_TILE = LANES // HEAD_DIM
N_HEAD_TILES = ATTN_WIDTH // LANES
TOKEN_TILE = 512
FF_CHUNK = 1024
VMEM_LIMIT = 56 * 1024 * 1024

SB_Q = 2 * Q_BLOCK
SB_EXTRA = 2
SB_EXIT = 104.0
LOG2E = 1.4426950408889634
FOX_Q = 512
FOX_K = 256

assert FOX_K == 2 * LANES and FOX_Q == 2 * FOX_K
assert SB_Q == 2 * Q_BLOCK and SB_EXTRA <= SB_Q // Q_BLOCK

F32 = jnp.float32
BF16 = jnp.bfloat16


def _params(n_axes):
    return pltpu.CompilerParams(dimension_semantics=("parallel",) * n_axes,
                                vmem_limit_bytes=VMEM_LIMIT)


def _const_spec(shape):
    nd = len(shape)
    return pl.BlockSpec(shape, lambda *_: (0,) * nd, pipeline_mode=pl.Buffered(1))


def _rms(x, g):
    ms = jnp.mean(x * x, axis=-1, keepdims=True)
    return (x * lax.rsqrt(ms + RMS_EPS)) * g


def _dot(a, b):
    return jnp.dot(a, b, preferred_element_type=F32)


def _dot_t(a, b):
    return lax.dot_general(a, b, (((1,), (1,)), ((), ())), preferred_element_type=F32)


def _head_masks(rows=Q_BLOCK):
    lane = lax.broadcasted_iota(jnp.int32, (rows, LANES), 1)
    m0 = jnp.where(lane < HEAD_DIM, 1.0, 0.0).astype(BF16)
    m1 = jnp.where(lane < HEAD_DIM, 0.0, 1.0).astype(BF16)
    return m0, m1


def _inproj_kernel(x_ref, g_ref, w_ref, o_ref, h_ref, *, n_out):
    h_ref[...] = _rms(x_ref[...], g_ref[...]).astype(BF16)
    for c in range(0, n_out, FF_CHUNK):
        o_ref[:, c:c + FF_CHUNK] = _dot(h_ref[...], w_ref[:, c:c + FF_CHUNK]).astype(BF16)


def _inproj(x2d, g, w):
    t, d = x2d.shape
    n_out = w.shape[1]
    return pl.pallas_call(
        functools.partial(_inproj_kernel, n_out=n_out),
        grid=(t // TOKEN_TILE,),
        in_specs=[pl.BlockSpec((TOKEN_TILE, d), lambda i: (i, 0)),
                  _const_spec((1, d)), _const_spec((d, n_out))],
        out_specs=pl.BlockSpec((TOKEN_TILE, n_out), lambda i: (i, 0)),
        out_shape=jax.ShapeDtypeStruct((t, n_out), BF16),
        scratch_shapes=[pltpu.VMEM((TOKEN_TILE, d), BF16)],
        compiler_params=_params(1),
        name="inproj",
    )(x2d, g, w)


def _inproj_fox_kernel(x_ref, g_ref, w_ref, wf_ref, o_ref, f_ref, h_ref, *, n_out):
    h_ref[...] = _rms(x_ref[...], g_ref[...]).astype(BF16)
    for c in range(0, n_out, FF_CHUNK):
        o_ref[:, c:c + FF_CHUNK] = _dot(h_ref[...], w_ref[:, c:c + FF_CHUNK]).astype(BF16)
    f_ref[...] = _dot(h_ref[...], wf_ref[...])


def _inproj_fox(x2d, g, w, wf):
    t, d = x2d.shape
    n_out = w.shape[1]
    return pl.pallas_call(
        functools.partial(_inproj_fox_kernel, n_out=n_out),
        grid=(t // TOKEN_TILE,),
        in_specs=[pl.BlockSpec((TOKEN_TILE, d), lambda i: (i, 0)),
                  _const_spec((1, d)), _const_spec((d, n_out)), _const_spec((d, LANES))],
        out_specs=[pl.BlockSpec((TOKEN_TILE, n_out), lambda i: (i, 0)),
                   pl.BlockSpec((TOKEN_TILE, LANES), lambda i: (i, 0))],
        out_shape=[jax.ShapeDtypeStruct((t, n_out), BF16),
                   jax.ShapeDtypeStruct((t, LANES), F32)],
        scratch_shapes=[pltpu.VMEM((TOKEN_TILE, d), BF16)],
        compiler_params=_params(1),
        name="inproj_fox",
    )(x2d, g, w, wf)


def _inproj_swa_kernel(x_ref, g_ref, w_ref, c_ref, s1_ref, s2_ref, o_ref, h_ref, *, n_rope, n_out):
    h_ref[...] = _rms(x_ref[...], g_ref[...]).astype(BF16)
    cos, s1, s2 = c_ref[...], s1_ref[...], s2_ref[...]
    for c in range(0, n_out, LANES):
        y = _dot(h_ref[...], w_ref[:, c:c + LANES])
        if c < n_rope:
            y = y * cos + pltpu.roll(y, 8, 1) * s1 + pltpu.roll(y, LANES - 8, 1) * s2
        o_ref[:, c:c + LANES] = y.astype(BF16)


def _inproj_swa(x2d, g, w, cos, s1, s2, seq):
    t, d = x2d.shape
    n_out = w.shape[1]
    n_rope = ATTN_WIDTH + SWA_KV_HEADS * HEAD_DIM
    tiles_per_seq = seq // TOKEN_TILE
    tab = pl.BlockSpec((TOKEN_TILE, LANES), lambda i: (i % tiles_per_seq, 0))
    return pl.pallas_call(
        functools.partial(_inproj_swa_kernel, n_rope=n_rope, n_out=n_out),
        grid=(t // TOKEN_TILE,),
        in_specs=[pl.BlockSpec((TOKEN_TILE, d), lambda i: (i, 0)),
                  _const_spec((1, d)), _const_spec((d, n_out)), tab, tab, tab],
        out_specs=pl.BlockSpec((TOKEN_TILE, n_out), lambda i: (i, 0)),
        out_shape=jax.ShapeDtypeStruct((t, n_out), BF16),
        scratch_shapes=[pltpu.VMEM((TOKEN_TILE, d), BF16)],
        compiler_params=_params(1),
        name="inproj_swa",
    )(x2d, g, w, cos, s1, s2)


def _rope_tables(seq):
    half = ROPE_DIM // 2
    inv_freq = ROPE_THETA ** (-jnp.arange(half, dtype=F32) / half)
    ang = jnp.arange(seq, dtype=jnp.int32).astype(F32)[:, None] * inv_freq[None, :]
    cos, sin = jnp.cos(ang), jnp.sin(ang)
    one = jnp.ones((seq, HEAD_DIM - ROPE_DIM), F32)
    zero = jnp.zeros((seq, HEAD_DIM - ROPE_DIM), F32)
    zh = jnp.zeros((seq, half), F32)
    c_head = jnp.concatenate([cos, cos, one], axis=1)
    s1_head = jnp.concatenate([zh, sin, zero], axis=1)
    s2_head = jnp.concatenate([-sin, zh, zero], axis=1)
    tile = lambda a: jnp.concatenate([a] * HEADS_PER_TILE, axis=1)
    return tile(c_head), tile(s1_head), tile(s2_head)


def _sb_kernel(q_ref, k_ref, v_ref, uu_ref, o_ref, ks_ref, vs_ref, acc_ref, r_ref, *, n_blocks):
    sel = _head_masks()
    blocks_per_q = SB_Q // Q_BLOCK

    def stack_heads(j, _):
        rows = pl.ds(pl.multiple_of(j * Q_BLOCK, Q_BLOCK), Q_BLOCK)
        kj, vj = k_ref[rows, :], v_ref[rows, :]
        ks_ref[j] = jnp.concatenate([kj * sel[0], kj * sel[1]], axis=0)
        vs_ref[j] = jnp.concatenate([vj * sel[0], vj * sel[1]], axis=0)
        return 0

    lax.fori_loop(0, n_blocks * blocks_per_q, stack_heads, 0)

    def strict(n_rows):
        return (lax.broadcasted_iota(jnp.int32, (n_rows, Q_BLOCK), 1)
                < lax.broadcasted_iota(jnp.int32, (n_rows, Q_BLOCK), 0))

    def key_block(q, j, r_in, keep):
        z2 = _dot_t(q, ks_ref[j])
        ws, r_out = [], []
        for h in range(HEADS_PER_TILE):
            z = z2[:, h * Q_BLOCK:(h + 1) * Q_BLOCK]
            sp = jnp.maximum(z, 0.0) + jnp.log(1.0 + jnp.exp2(jnp.abs(z) * -LOG2E))
            spm = sp if keep is None else jnp.where(keep, sp, 0.0)
            hi = spm.astype(BF16)
            lo = (spm - hi.astype(F32)).astype(BF16)
            cs = _dot(jnp.concatenate([hi, lo], axis=1), uu_ref[...])
            tail, tot = cs[:, :Q_BLOCK], cs[:, Q_BLOCK:]
            if r_in is not None:
                tail, tot = tail + r_in[h], tot + r_in[h]
            w = jnp.exp2((z - tail) * LOG2E)
            if keep is not None:
                w = jnp.where(keep, w, 0.0)
            ws.append(w.astype(BF16))
            r_out.append(tot)
        return _dot(jnp.concatenate(ws, axis=1), vs_ref[j]), r_out

    def q_block(i, n_extra):
        static_i = isinstance(i, int)
        q_off = i * SB_Q if static_i else pl.multiple_of(i * SB_Q, SB_Q)
        j_diag = i * blocks_per_q
        scale = jnp.asarray(SCALE, BF16)
        q = q_ref[pl.ds(q_off, SB_Q), :] * scale
        zeros = jnp.zeros((Q_BLOCK, LANES), F32)
        pv_lo, r_lo = key_block(q_ref[pl.ds(q_off + Q_BLOCK, Q_BLOCK), :] * scale,
                                j_diag + 1, None, strict(Q_BLOCK))
        r = [jnp.concatenate([zeros, r_lo[h]], axis=0) for h in range(HEADS_PER_TILE)]
        pv, r = key_block(q, j_diag, r, strict(SB_Q))
        acc = pv + jnp.concatenate([zeros, pv_lo], axis=0)
        for d in range(1, n_extra + 1):
            pv, r = key_block(q, j_diag - d, r, None)
            acc = acc + pv
        if static_i and j_diag <= n_extra:
            o_ref[pl.ds(q_off, SB_Q), :] = acc.astype(BF16)
            return
        acc_ref[...] = acc
        for h in range(HEADS_PER_TILE):
            r_ref[h] = r[h]

        def more(c):
            j, r_min = c
            return (j >= 0) & (r_min <= SB_EXIT)

        def step(c):
            j, _ = c
            pv, r_new = key_block(q, j, [r_ref[h] for h in range(HEADS_PER_TILE)], None)
            acc_ref[...] += pv
            for h in range(HEADS_PER_TILE):
                r_ref[h] = r_new[h]
            return j - 1, jnp.min(jnp.minimum(r_new[0], r_new[1]))

        lax.while_loop(more, step, (j_diag - n_extra - 1, jnp.min(jnp.minimum(r[0], r[1]))))
        o_ref[pl.ds(q_off, SB_Q), :] = acc_ref[...].astype(BF16)

    q_block(0, 0)

    def body(i, _):
        q_block(i, SB_EXTRA)
        return 0

    lax.fori_loop(1, n_blocks, body, 0)


def _cumsum_weights():
    a = np.arange(Q_BLOCK)
    u = (a[:, None] >= a[None, :]).astype(np.float32)
    half = np.concatenate([u, np.ones_like(u)], axis=1)
    return jnp.asarray(np.concatenate([half, half], axis=0), BF16)


def _sb_attention(qkv, batch, seq):
    blk = lambda off: pl.BlockSpec((None, seq, LANES), lambda b, t: (b, 0, off + t))
    return pl.pallas_call(
        functools.partial(_sb_kernel, n_blocks=seq // SB_Q),
        grid=(batch, N_HEAD_TILES),
        in_specs=[blk(0), blk(N_HEAD_TILES), blk(2 * N_HEAD_TILES),
                  _const_spec((2 * Q_BLOCK, 2 * Q_BLOCK))],
        out_specs=pl.BlockSpec((None, seq, LANES), lambda b, t: (b, 0, t)),
        out_shape=jax.ShapeDtypeStruct((batch, seq, ATTN_WIDTH), BF16),
        scratch_shapes=[pltpu.VMEM((seq // Q_BLOCK, 2 * Q_BLOCK, LANES), BF16),
                        pltpu.VMEM((seq // Q_BLOCK, 2 * Q_BLOCK, LANES), BF16),
                        pltpu.VMEM((SB_Q, LANES), F32),
                        pltpu.VMEM((HEADS_PER_TILE, SB_Q, Q_BLOCK), F32)],
        compiler_params=_params(2),
        name="sb_attention",
    )(qkv, qkv, qkv, _cumsum_weights())


def _fox_cum_kernel(f_ref, b_ref, tri_ref, o_ref, *, n_blocks):
    x = f_ref[...] + b_ref[...]
    lf = jnp.minimum(x, 0.0) - jnp.log1p(jnp.exp(-jnp.abs(x)))
    lft = lf.T
    carry = jnp.zeros((LANES, 1), F32)
    for c in range(n_blocks):
        blk = lft[:, c * Q_BLOCK:(c + 1) * Q_BLOCK]
        hi = blk.astype(BF16)
        r1 = blk - hi.astype(F32)
        mid = r1.astype(BF16)
        lo = (r1 - mid.astype(F32)).astype(BF16)
        cs = _dot(hi, tri_ref[...]) + _dot(mid, tri_ref[...]) + _dot(lo, tri_ref[...]) + carry
        o_ref[:, c * Q_BLOCK:(c + 1) * Q_BLOCK] = cs[:N_HEADS]
        carry = cs[:, Q_BLOCK - 1:Q_BLOCK]


def _fox_cum(flog, b_pad, batch, seq):
    a = np.arange(Q_BLOCK)
    tri = jnp.asarray((a[:, None] <= a[None, :]).astype(np.float32), BF16)
    return pl.pallas_call(
        functools.partial(_fox_cum_kernel, n_blocks=seq // Q_BLOCK),
        grid=(batch,),
        in_specs=[pl.BlockSpec((None, seq, LANES), lambda b: (b, 0, 0)),
                  _const_spec((1, LANES)), _const_spec((Q_BLOCK, Q_BLOCK))],
        out_specs=pl.BlockSpec((None, N_HEADS, seq), lambda b: (b, 0, 0)),
        out_shape=jax.ShapeDtypeStruct((batch, N_HEADS, seq), F32),
        compiler_params=_params(1),
        name="fox_cum",
    )(flog, b_pad, tri)


def _fox_kernel(q_ref, k_ref, v_ref, c_ref, o_ref, ks_ref, vs_ref, sa_ref, sb_ref, acc_ref, l_ref, m_ref,
                *, n_blocks):
    sel_k = _head_masks(FOX_K)
    k_per_q = FOX_Q // FOX_K

    def stack_heads(j, _):
        rows = pl.ds(pl.multiple_of(j * FOX_K, FOX_K), FOX_K)
        kj, vj = k_ref[rows, :], v_ref[rows, :]
        ks_ref[j] = jnp.concatenate([kj * sel_k[0], kj * sel_k[1]], axis=0)
        vs_ref[j] = jnp.concatenate(
            [jnp.concatenate([vj * sel_k[0], sel_k[0]], axis=1),
             jnp.concatenate([vj * sel_k[1], sel_k[1]], axis=1)], axis=0)
        return 0

    lax.fori_loop(0, n_blocks * k_per_q, stack_heads, 0)

    def scores(q_off, j, r0):
        q = q_ref[pl.ds(q_off + r0, FOX_Q - r0), :] * jnp.asarray(SCALE, BF16)
        return _dot_t(q, ks_ref[j])

    def consume(s2, q_off, j, r0, masked):
        rows = slice(r0, FOX_Q)
        n_rows = FOX_Q - r0
        head0 = lax.broadcasted_iota(jnp.int32, (n_rows, LANES), 1) < HEAD_DIM
        k_off = pl.multiple_of(j * FOX_K, FOX_K)
        ps, alphas = [], []
        for h in range(HEADS_PER_TILE):
            cj = c_ref[h, :, pl.ds(k_off, FOX_K)]
            s = s2[:, h * FOX_K:(h + 1) * FOX_K] - cj
            if masked:
                col_minus_row = (lax.broadcasted_iota(jnp.int32, (n_rows, FOX_K), 1)
                                 - lax.broadcasted_iota(jnp.int32, (n_rows, FOX_K), 0))
                s = jnp.where(col_minus_row <= q_off + r0 - k_off, s, NEG_INF)
            m_old = m_ref[h, rows]
            m_blk = jnp.max(jnp.maximum(s[:, :LANES], s[:, LANES:]), axis=-1, keepdims=True)
            m_new = jnp.maximum(m_old, jnp.broadcast_to(m_blk, (n_rows, LANES)))
            m_ref[h, rows] = m_new
            alphas.append(jnp.exp(m_old - m_new))
            ps.append(jnp.concatenate([jnp.exp(s[:, :LANES] - m_new),
                                       jnp.exp(s[:, LANES:] - m_new)], axis=1).astype(BF16))
        alpha = jnp.where(head0, alphas[0], alphas[1])
        pv = _dot(jnp.concatenate(ps, axis=1), vs_ref[j])
        acc_ref[rows] = acc_ref[rows] * alpha + pv[:, :LANES]
        l_ref[rows] = l_ref[rows] * alpha + pv[:, LANES:]

    def q_block(i, _):
        q_off = pl.multiple_of(i * FOX_Q, FOX_Q)
        acc_ref[...] = jnp.zeros_like(acc_ref)
        l_ref[...] = jnp.zeros_like(l_ref)
        m_ref[...] = jnp.full_like(m_ref, NEG_INF)
        sa_ref[...] = scores(q_off, 0, 0)

        def two_blocks(p, _):
            j = 2 * p
            s2 = sa_ref[...]
            sb_ref[...] = scores(q_off, j + 1, 0)
            consume(s2, q_off, j, 0, False)
            s2 = sb_ref[...]
            sa_ref[...] = scores(q_off, j + 2, 0)
            consume(s2, q_off, j + 1, 0, False)
            return 0

        lax.fori_loop(0, i, two_blocks, 0)
        j_diag = i * k_per_q
        consume(sa_ref[...], q_off, j_diag, 0, True)
        for d in range(1, k_per_q):
            consume(scores(q_off, j_diag + d, d * FOX_K), q_off, j_diag + d, d * FOX_K, True)
        o_ref[pl.ds(q_off, FOX_Q), :] = (acc_ref[...] / l_ref[...]).astype(BF16)
        return 0

    lax.fori_loop(0, n_blocks, q_block, 0)


def _fox_attention(qkv, cum, batch, seq):
    blk = lambda off: pl.BlockSpec((None, seq, LANES), lambda b, t: (b, 0, off + t))
    return pl.pallas_call(
        functools.partial(_fox_kernel, n_blocks=seq // FOX_Q),
        grid=(batch, N_HEAD_TILES),
        in_specs=[blk(0), blk(N_HEAD_TILES), blk(2 * N_HEAD_TILES),
                  pl.BlockSpec((None, HEADS_PER_TILE, 1, seq), lambda b, t: (b, t, 0, 0))],
        out_specs=pl.BlockSpec((None, seq, LANES), lambda b, t: (b, 0, t)),
        out_shape=jax.ShapeDtypeStruct((batch, seq, ATTN_WIDTH), BF16),
        scratch_shapes=[pltpu.VMEM((seq // FOX_K, 2 * FOX_K, LANES), BF16),
                        pltpu.VMEM((seq // FOX_K, 2 * FOX_K, 2 * LANES), BF16),
                        pltpu.VMEM((FOX_Q, 2 * FOX_K), F32),
                        pltpu.VMEM((FOX_Q, 2 * FOX_K), F32),
                        pltpu.VMEM((FOX_Q, LANES), F32),
                        pltpu.VMEM((FOX_Q, LANES), F32),
                        pltpu.VMEM((HEADS_PER_TILE, FOX_Q, LANES), F32)],
        compiler_params=_params(2),
        name="fox_attention",
    )(qkv, qkv, qkv, cum)


def _swa_kernel(sink_ref, q_ref, kp_ref, kc_ref, vp_ref, vc_ref, o_ref):
    g = pl.program_id(1)
    n = pl.program_id(2)
    window = 2 * Q_BLOCK
    lane = lax.broadcasted_iota(jnp.int32, (window, LANES), 1)
    kw = jnp.concatenate([kp_ref[...], kc_ref[...]], axis=0).astype(F32)
    vw = jnp.concatenate([vp_ref[...], vc_ref[...]], axis=0).astype(F32)
    mine = (lane >= g * HEAD_DIM) & (lane < (g + 1) * HEAD_DIM)
    kg = jnp.where(mine, kw, 0.0)
    vg = jnp.where(mine, vw, 0.0)
    k2 = (kg + pltpu.roll(kg, HEAD_DIM, 1)).astype(BF16)
    v2 = (vg + pltpu.roll(vg, HEAD_DIM, 1)).astype(BF16)
    sel = _head_masks(window)
    kstack = jnp.concatenate([k2 * sel[0], k2 * sel[1]], axis=0)
    vext = jnp.concatenate([jnp.concatenate([v2 * sel[0], sel[0]], axis=1),
                            jnp.concatenate([v2 * sel[1], sel[1]], axis=1)], axis=0)

    a = lax.broadcasted_iota(jnp.int32, (Q_BLOCK, window), 0)
    c = lax.broadcasted_iota(jnp.int32, (Q_BLOCK, window), 1)
    mask = (c > a) & (c <= a + SWA_WINDOW) & (c + n * Q_BLOCK >= Q_BLOCK)
    head0 = lax.broadcasted_iota(jnp.int32, (Q_BLOCK, LANES), 1) < HEAD_DIM

    for t in range(SWA_GROUP // HEADS_PER_TILE):
        qt = q_ref[:, t * LANES:(t + 1) * LANES] * jnp.asarray(SCALE, BF16)
        s2 = _dot_t(qt, kstack)
        es, sink_terms = [], []
        for h in range(HEADS_PER_TILE):
            s = jnp.where(mask, s2[:, h * window:(h + 1) * window], NEG_INF)
            sink = sink_ref[g * SWA_GROUP + t * HEADS_PER_TILE + h]
            m = jnp.max(jnp.maximum(s[:, :LANES], s[:, LANES:]), axis=-1, keepdims=True)
            m = jnp.maximum(jnp.broadcast_to(m, (Q_BLOCK, LANES)), sink)
            es.append(jnp.concatenate([jnp.exp(s[:, :LANES] - m),
                                       jnp.exp(s[:, LANES:] - m)], axis=1).astype(BF16))
            sink_terms.append(jnp.exp(sink - m))
        pv = _dot(jnp.concatenate(es, axis=1), vext)
        den = pv[:, LANES:] + jnp.where(head0, sink_terms[0], sink_terms[1])
        o_ref[:, t * LANES:(t + 1) * LANES] = (pv[:, :LANES] / den).astype(BF16)


def _swa_attention(qkv, sinks, batch, seq):
    nb = seq // Q_BLOCK
    gw = SWA_GROUP * HEAD_DIM
    k_tile = ATTN_WIDTH // LANES
    v_tile = k_tile + 1
    prev = lambda tile: pl.BlockSpec((None, Q_BLOCK, LANES),
                                     lambda b, g, n, s: (b, jnp.maximum(n - 1, 0), tile))
    cur = lambda tile: pl.BlockSpec((None, Q_BLOCK, LANES), lambda b, g, n, s: (b, n, tile))
    grid_spec = pltpu.PrefetchScalarGridSpec(
        num_scalar_prefetch=1,
        grid=(batch, SWA_KV_HEADS, nb),
        in_specs=[pl.BlockSpec((None, Q_BLOCK, gw), lambda b, g, n, s: (b, n, g)),
                  prev(k_tile), cur(k_tile), prev(v_tile), cur(v_tile)],
        out_specs=pl.BlockSpec((None, Q_BLOCK, gw), lambda b, g, n, s: (b, n, g)),
    )
    return pl.pallas_call(
        _swa_kernel,
        grid_spec=grid_spec,
        out_shape=jax.ShapeDtypeStruct((batch, seq, ATTN_WIDTH), BF16),
        compiler_params=_params(3),
        name="swa_attention",
    )(sinks, qkv, qkv, qkv, qkv, qkv)


def _tail_kernel(x_ref, a_ref, p_ref, wo_ref, mg_ref, wu_ref, wd_ref, pg_ref, wg_ref, wp_ref,
                 fg_ref, o_ref, h_ref, *, final):
    x1 = x_ref[...] + _dot(a_ref[...], wo_ref[...])
    h_ref[...] = _rms(x1, mg_ref[...]).astype(BF16)
    x2 = x1
    for c in range(0, D_FF, FF_CHUNK):
        u = jnp.maximum(_dot(h_ref[...], wu_ref[:, c:c + FF_CHUNK]), 0.0)
        x2 = x2 + _dot((u * u).astype(BF16), wd_ref[c:c + FF_CHUNK, :])
    h3 = _rms(x2, pg_ref[...]).astype(BF16)
    gate = 1.0 / (1.0 + jnp.exp(-_dot(h3, wg_ref[...])))
    x3 = x2 + _dot(p_ref[...].astype(BF16), wp_ref[...]) * gate
    o_ref[...] = _rms(x3, fg_ref[...]) if final else x3


def _tail(x2d, a2d, p2d, wo, mg, wu, wd, pg, wg, wp, fg, final):
    t, d = x2d.shape
    tok = lambda w: pl.BlockSpec((TOKEN_TILE, w), lambda i: (i, 0))
    return pl.pallas_call(
        functools.partial(_tail_kernel, final=final),
        grid=(t // TOKEN_TILE,),
        in_specs=[tok(d), tok(ATTN_WIDTH), tok(PLE_DIM),
                  _const_spec((ATTN_WIDTH, d)), _const_spec((1, d)),
                  _const_spec((d, D_FF)), _const_spec((D_FF, d)), _const_spec((1, d)),
                  _const_spec((d, d)), _const_spec((PLE_DIM, d)), _const_spec((1, d))],
        out_specs=tok(d),
        out_shape=jax.ShapeDtypeStruct((t, d), F32),
        scratch_shapes=[pltpu.VMEM((TOKEN_TILE, d), BF16)],
        compiler_params=_params(1),
        name="layer_tail",
    )(x2d, a2d, p2d, wo, mg, wu, wd, pg, wg, wp, fg)


def kernel(x, p, attn_norm_0, w_in_0, w_out_0, mlp_norm_0, w_up_0, w_down_0, ple_norm_0, w_ple_gate_0, w_ple_proj_0, attn_norm_1, w_in_1, w_out_1, sinks_1, mlp_norm_1, w_up_1, w_down_1, ple_norm_1, w_ple_gate_1, w_ple_proj_1, attn_norm_2, w_in_2, w_out_2, b_forget_2, mlp_norm_2, w_up_2, w_down_2, ple_norm_2, w_ple_gate_2, w_ple_proj_2, attn_norm_3, w_in_3, w_out_3, mlp_norm_3, w_up_3, w_down_3, ple_norm_3, w_ple_gate_3, w_ple_proj_3, final_norm):
    batch, seq, d = x.shape
    t = batch * seq
    layers = [
        (attn_norm_0, w_in_0, w_out_0, None, mlp_norm_0, w_up_0, w_down_0, ple_norm_0, w_ple_gate_0, w_ple_proj_0),
        (attn_norm_1, w_in_1, w_out_1, sinks_1, mlp_norm_1, w_up_1, w_down_1, ple_norm_1, w_ple_gate_1, w_ple_proj_1),
        (attn_norm_2, w_in_2, w_out_2, b_forget_2, mlp_norm_2, w_up_2, w_down_2, ple_norm_2, w_ple_gate_2, w_ple_proj_2),
        (attn_norm_3, w_in_3, w_out_3, None, mlp_norm_3, w_up_3, w_down_3, ple_norm_3, w_ple_gate_3, w_ple_proj_3),
    ]
    row = lambda g: g.reshape(1, d)
    bf = lambda w: w.astype(BF16)
    x2d = x.reshape(t, d)
    fg = row(final_norm)
    for i, (an, wi, wo, extra, mn, wu, wd, pn, wg, wp) in enumerate(layers):
        kind = i % 3
        if kind == 0:
            qkv = _inproj(x2d, row(an), bf(wi))
            a = _sb_attention(qkv.reshape(batch, seq, -1), batch, seq)
        elif kind == 1:
            cos, s1, s2 = _rope_tables(seq)
            qkv = _inproj_swa(x2d, row(an), bf(wi), cos, s1, s2, seq)
            a = _swa_attention(qkv.reshape(batch, seq, -1), extra.astype(F32), batch, seq)
        else:
            wf = jnp.pad(wi[:, 3 * ATTN_WIDTH:], ((0, 0), (0, LANES - N_HEADS)))
            qkv, flog = _inproj_fox(x2d, row(an), bf(wi[:, :3 * ATTN_WIDTH]), bf(wf))
            b_pad = jnp.pad(extra.astype(F32), (0, LANES - N_HEADS)).reshape(1, LANES)
            cum = _fox_cum(flog.reshape(batch, seq, LANES), b_pad, batch, seq)
            a = _fox_attention(qkv.reshape(batch, seq, -1),
                               cum.reshape(batch, N_HEADS, 1, seq), batch, seq)
        x2d = _tail(x2d, a.reshape(t, ATTN_WIDTH), p[i].reshape(t, PLE_DIM),
                    bf(wo), row(mn), bf(wu), bf(wd), row(pn), bf(wg), bf(wp), fg,
                    final=(i == len(layers) - 1))
    return x2d.reshape(batch, seq, d)
```

```python
import functools

import jax
import jax.numpy as jnp
import numpy as np
from jax import lax
from jax.experimental import pallas as pl
from jax.experimental.pallas import tpu as pltpu

D_MODEL = 1024
HEAD_DIM = 64
N_HEADS = 16
ATTN_WIDTH = N_HEADS * HEAD_DIM
SWA_KV_HEADS = 2
SWA_GROUP = N_HEADS // SWA_KV_HEADS
SWA_WINDOW = 128
ROPE_THETA = 500000.0
ROPE_DIM = 16
D_FF = 4 * D_MODEL
PLE_DIM = 256
Q_BLOCK = 128
RMS_EPS = 1e-6
NEG_INF = -1e30
SCALE = HEAD_DIM ** -0.5

LANES = 128
HEADS_PER_TILE = LANES // HEAD_DIM
N_HEAD_TILES = ATTN_WIDTH // LANES
TOKEN_TILE = 512
FF_CHUNK = 1024
VMEM_LIMIT = 56 * 1024 * 1024

SB_Q = 2 * Q_BLOCK
SB_EXTRA = 2
SB_EXIT = 104.0
LOG2E = 1.4426950408889634
FOX_Q = 512
FOX_K = 256

assert FOX_K == 2 * LANES and FOX_Q == 2 * FOX_K
assert SB_Q == 2 * Q_BLOCK and SB_EXTRA <= SB_Q // Q_BLOCK

F32 = jnp.float32
BF16 = jnp.bfloat16


def _params(n_axes):
    return pltpu.CompilerParams(dimension_semantics=("parallel",) * n_axes,
                                vmem_limit_bytes=VMEM_LIMIT)


def _const_spec(shape):
    nd = len(shape)
    return pl.BlockSpec(shape, lambda *_: (0,) * nd, pipeline_mode=pl.Buffered(1))


def _rms(x, g):
    ms = jnp.mean(x * x, axis=-1, keepdims=True)
    return (x * lax.rsqrt(ms + RMS_EPS)) * g


def _dot(a, b):
    return jnp.dot(a, b, preferred_element_type=F32)


def _dot_t(a, b):
    return lax.dot_general(a, b, (((1,), (1,)), ((), ())), preferred_element_type=F32)


def _head_masks(rows=Q_BLOCK):
    lane = lax.broadcasted_iota(jnp.int32, (rows, LANES), 1)
    m0 = jnp.where(lane < HEAD_DIM, 1.0, 0.0).astype(BF16)
    m1 = jnp.where(lane < HEAD_DIM, 0.0, 1.0).astype(BF16)
    return m0, m1


def _inproj_kernel(x_ref, g_ref, w_ref, o_ref, h_ref, *, n_out):
    h_ref[...] = _rms(x_ref[...], g_ref[...]).astype(BF16)
    for c in range(0, n_out, FF_CHUNK):
        o_ref[:, c:c + FF_CHUNK] = _dot(h_ref[...], w_ref[:, c:c + FF_CHUNK]).astype(BF16)


def _inproj(x2d, g, w):
    t, d = x2d.shape
    n_out = w.shape[1]
    return pl.pallas_call(
        functools.partial(_inproj_kernel, n_out=n_out),
        grid=(t // TOKEN_TILE,),
        in_specs=[pl.BlockSpec((TOKEN_TILE, d), lambda i: (i, 0)),
                  _const_spec((1, d)), _const_spec((d, n_out))],
        out_specs=pl.BlockSpec((TOKEN_TILE, n_out), lambda i: (i, 0)),
        out_shape=jax.ShapeDtypeStruct((t, n_out), BF16),
        scratch_shapes=[pltpu.VMEM((TOKEN_TILE, d), BF16)],
        compiler_params=_params(1),
        name="inproj",
    )(x2d, g, w)


def _inproj_fox_kernel(x_ref, g_ref, w_ref, wf_ref, o_ref, f_ref, h_ref, *, n_out):
    h_ref[...] = _rms(x_ref[...], g_ref[...]).astype(BF16)
    for c in range(0, n_out, FF_CHUNK):
        o_ref[:, c:c + FF_CHUNK] = _dot(h_ref[...], w_ref[:, c:c + FF_CHUNK]).astype(BF16)
    f_ref[...] = _dot(h_ref[...], wf_ref[...])


def _inproj_fox(x2d, g, w, wf):
    t, d = x2d.shape
    n_out = 3 * ATTN_WIDTH
    return pl.pallas_call(
        functools.partial(_inproj_fox_kernel, n_out=n_out),
        grid=(t // TOKEN_TILE,),
        in_specs=[pl.BlockSpec((TOKEN_TILE, d), lambda i: (i, 0)),
                  _const_spec((1, d)), _const_spec(w.shape), _const_spec((d, LANES))],
        out_specs=[pl.BlockSpec((TOKEN_TILE, n_out), lambda i: (i, 0)),
                   pl.BlockSpec((TOKEN_TILE, LANES), lambda i: (i, 0))],
        out_shape=[jax.ShapeDtypeStruct((t, n_out), BF16),
                   jax.ShapeDtypeStruct((t, LANES), F32)],
        scratch_shapes=[pltpu.VMEM((TOKEN_TILE, d), BF16)],
        compiler_params=_params(1),
        name="inproj_fox",
    )(x2d, g, w, wf)


def _inproj_swa_kernel(x_ref, g_ref, w_ref, c_ref, s1_ref, s2_ref, o_ref, h_ref, *, n_rope, n_out):
    h_ref[...] = _rms(x_ref[...], g_ref[...]).astype(BF16)
    cos, s1, s2 = c_ref[...], s1_ref[...], s2_ref[...]
    for c in range(0, n_out, LANES):
        y = _dot(h_ref[...], w_ref[:, c:c + LANES])
        if c < n_rope:
            y = y * cos + pltpu.roll(y, 8, 1) * s1 + pltpu.roll(y, LANES - 8, 1) * s2
        o_ref[:, c:c + LANES] = y.astype(BF16)


def _inproj_swa(x2d, g, w, cos, s1, s2, seq):
    t, d = x2d.shape
    n_out = w.shape[1]
    n_rope = ATTN_WIDTH + SWA_KV_HEADS * HEAD_DIM
    tiles_per_seq = seq // TOKEN_TILE
    tab = pl.BlockSpec((TOKEN_TILE, LANES), lambda i: (i % tiles_per_seq, 0))
    return pl.pallas_call(
        functools.partial(_inproj_swa_kernel, n_rope=n_rope, n_out=n_out),
        grid=(t // TOKEN_TILE,),
        in_specs=[pl.BlockSpec((TOKEN_TILE, d), lambda i: (i, 0)),
                  _const_spec((1, d)), _const_spec((d, n_out)), tab, tab, tab],
        out_specs=pl.BlockSpec((TOKEN_TILE, n_out), lambda i: (i, 0)),
        out_shape=jax.ShapeDtypeStruct((t, n_out), BF16),
        scratch_shapes=[pltpu.VMEM((TOKEN_TILE, d), BF16)],
        compiler_params=_params(1),
        name="inproj_swa",
    )(x2d, g, w, cos, s1, s2)


def _rope_tables(seq):
    half = ROPE_DIM // 2
    inv_freq = ROPE_THETA ** (-jnp.arange(half, dtype=F32) / half)
    ang = jnp.arange(seq, dtype=jnp.int32).astype(F32)[:, None] * inv_freq[None, :]
    cos, sin = jnp.cos(ang), jnp.sin(ang)
    one = jnp.ones((seq, HEAD_DIM - ROPE_DIM), F32)
    zero = jnp.zeros((seq, HEAD_DIM - ROPE_DIM), F32)
    zh = jnp.zeros((seq, half), F32)
    c_head = jnp.concatenate([cos, cos, one], axis=1)
    s1_head = jnp.concatenate([zh, sin, zero], axis=1)
    s2_head = jnp.concatenate([-sin, zh, zero], axis=1)
    tile = lambda a: jnp.concatenate([a] * HEADS_PER_TILE, axis=1)
    return tile(c_head), tile(s1_head), tile(s2_head)


def _sb_kernel(q_ref, k_ref, v_ref, uu_ref, o_ref, ks_ref, vs_ref, za_ref, zb_ref, acc_ref, r_ref,
               *, n_blocks):
    sel = _head_masks()
    blocks_per_q = SB_Q // Q_BLOCK
    n_static = blocks_per_q + SB_EXTRA
    scale = jnp.asarray(SCALE, BF16)

    def stack_heads(j, _):
        rows = pl.ds(pl.multiple_of(j * Q_BLOCK, Q_BLOCK), Q_BLOCK)
        kj, vj = k_ref[rows, :], v_ref[rows, :]
        ks_ref[j] = jnp.concatenate([kj * sel[0], kj * sel[1]], axis=0)
        vs_ref[j] = jnp.concatenate([vj * sel[0], vj * sel[1]], axis=0)
        return 0

    lax.fori_loop(0, n_blocks * blocks_per_q, stack_heads, 0)

    def strict(n_rows):
        return (lax.broadcasted_iota(jnp.int32, (n_rows, Q_BLOCK), 1)
                < lax.broadcasted_iota(jnp.int32, (n_rows, Q_BLOCK), 0))

    def scores_ahead(i):
        q = q_ref[pl.ds(pl.multiple_of(i * SB_Q, SB_Q), SB_Q), :] * scale
        ks = ks_ref[pl.ds(i * blocks_per_q - SB_EXTRA, n_static)]
        return _dot_t(q, ks.reshape(n_static * 2 * Q_BLOCK, LANES))

    def key_block(z2, j, r_in, keep):
        ws, r_out = [], []
        for h in range(HEADS_PER_TILE):
            z = z2[:, h * Q_BLOCK:(h + 1) * Q_BLOCK]
            sp = jnp.maximum(z, 0.0) + jnp.log(1.0 + jnp.exp2(jnp.abs(z) * -LOG2E))
            spm = sp if keep is None else jnp.where(keep, sp, 0.0)
            hi = spm.astype(BF16)
            lo = (spm - hi.astype(F32)).astype(BF16)
            cs = _dot(jnp.concatenate([hi, lo], axis=1), uu_ref[...])
            tail, tot = cs[:, :Q_BLOCK], cs[:, Q_BLOCK:]
            if r_in is not None:
                tail, tot = tail + r_in[h], tot + r_in[h]
            w = jnp.exp2((z - tail) * LOG2E)
            if keep is not None:
                w = jnp.where(keep, w, 0.0)
            ws.append(w.astype(BF16))
            r_out.append(tot)
        return _dot(jnp.concatenate(ws, axis=1), vs_ref[j]), r_out

    def q_block(i, n_extra, z_ref):
        static_i = isinstance(i, int)
        q_off = i * SB_Q if static_i else pl.multiple_of(i * SB_Q, SB_Q)
        j_diag = i * blocks_per_q
        q = q_ref[pl.ds(q_off, SB_Q), :] * scale

        def z_of(d, r0=0):
            if z_ref is None:
                return _dot_t(q_ref[pl.ds(q_off + r0, SB_Q - r0), :] * scale, ks_ref[j_diag - d])
            c0 = (SB_EXTRA - d) * 2 * Q_BLOCK
            return z_ref[r0:, c0:c0 + 2 * Q_BLOCK]

        zeros = jnp.zeros((Q_BLOCK, LANES), F32)
        pv_lo, r_lo = key_block(z_of(-1, Q_BLOCK), j_diag + 1, None, strict(Q_BLOCK))
        r = [jnp.concatenate([zeros, r_lo[h]], axis=0) for h in range(HEADS_PER_TILE)]
        pv, r = key_block(z_of(0), j_diag, r, strict(SB_Q))
        acc = pv + jnp.concatenate([zeros, pv_lo], axis=0)
        for d in range(1, n_extra + 1):
            pv, r = key_block(z_of(d), j_diag - d, r, None)
            acc = acc + pv
        if static_i and j_diag <= n_extra:
            o_ref[pl.ds(q_off, SB_Q), :] = acc.astype(BF16)
            return
        acc_ref[...] = acc
        for h in range(HEADS_PER_TILE):
            r_ref[h] = r[h]

        def more(c):
            j, r_min = c
            return (j >= 0) & (r_min <= SB_EXIT)

        def step(c):
            j, _ = c
            pv, r_new = key_block(_dot_t(q, ks_ref[j]), j,
                                  [r_ref[h] for h in range(HEADS_PER_TILE)], None)
            acc_ref[...] += pv
            for h in range(HEADS_PER_TILE):
                r_ref[h] = r_new[h]
            return j - 1, jnp.min(jnp.minimum(r_new[0], r_new[1]))

        lax.while_loop(more, step, (j_diag - n_extra - 1, jnp.min(jnp.minimum(r[0], r[1]))))
        o_ref[pl.ds(q_off, SB_Q), :] = acc_ref[...].astype(BF16)

    q_block(0, 0, None)
    za_ref[...] = scores_ahead(1)

    def two_blocks(p, _):
        i = 2 * p + 1
        zb_ref[...] = scores_ahead(i + 1)
        q_block(i, SB_EXTRA, za_ref)
        za_ref[...] = scores_ahead(i + 2)
        q_block(i + 1, SB_EXTRA, zb_ref)
        return 0

    lax.fori_loop(0, (n_blocks - 2) // 2, two_blocks, 0)
    q_block(n_blocks - 1, SB_EXTRA, za_ref)


def _cumsum_weights():
    a = np.arange(Q_BLOCK)
    u = (a[:, None] >= a[None, :]).astype(np.float32)
    half = np.concatenate([u, np.ones_like(u)], axis=1)
    return jnp.asarray(np.concatenate([half, half], axis=0), BF16)


def _sb_attention(qkv, batch, seq):
    assert seq % (2 * SB_Q) == 0, "query blocks after the first are processed in pairs plus one"
    blk = lambda off: pl.BlockSpec((None, seq, LANES), lambda b, t: (b, 0, off + t))
    return pl.pallas_call(
        functools.partial(_sb_kernel, n_blocks=seq // SB_Q),
        grid=(batch, N_HEAD_TILES),
        in_specs=[blk(0), blk(N_HEAD_TILES), blk(2 * N_HEAD_TILES),
                  _const_spec((2 * Q_BLOCK, 2 * Q_BLOCK))],
        out_specs=pl.BlockSpec((None, seq, LANES), lambda b, t: (b, 0, t)),
        out_shape=jax.ShapeDtypeStruct((batch, seq, ATTN_WIDTH), BF16),
        scratch_shapes=[pltpu.VMEM((seq // Q_BLOCK, 2 * Q_BLOCK, LANES), BF16),
                        pltpu.VMEM((seq // Q_BLOCK, 2 * Q_BLOCK, LANES), BF16),
                        pltpu.VMEM((SB_Q, (SB_Q // Q_BLOCK + SB_EXTRA) * 2 * Q_BLOCK), F32),
                        pltpu.VMEM((SB_Q, (SB_Q // Q_BLOCK + SB_EXTRA) * 2 * Q_BLOCK), F32),
                        pltpu.VMEM((SB_Q, LANES), F32),
                        pltpu.VMEM((HEADS_PER_TILE, SB_Q, Q_BLOCK), F32)],
        compiler_params=_params(2),
        name="sb_attention",
    )(qkv, qkv, qkv, _cumsum_weights())


def _fox_cum_kernel(f_ref, b_ref, tri_ref, o_ref, *, n_blocks):
    x = f_ref[...] + b_ref[...]
    lf = jnp.minimum(x, 0.0) - jnp.log1p(jnp.exp(-jnp.abs(x)))
    lft = lf.T
    carry = jnp.zeros((LANES, 1), F32)
    for c in range(n_blocks):
        blk = lft[:, c * Q_BLOCK:(c + 1) * Q_BLOCK]
        hi = blk.astype(BF16)
        r1 = blk - hi.astype(F32)
        mid = r1.astype(BF16)
        lo = (r1 - mid.astype(F32)).astype(BF16)
        cs = _dot(hi, tri_ref[...]) + _dot(mid, tri_ref[...]) + _dot(lo, tri_ref[...]) + carry
        o_ref[:, c * Q_BLOCK:(c + 1) * Q_BLOCK] = cs[:N_HEADS]
        carry = cs[:, Q_BLOCK - 1:Q_BLOCK]


def _fox_cum(flog, b_pad, batch, seq):
    a = np.arange(Q_BLOCK)
    tri = jnp.asarray((a[:, None] <= a[None, :]).astype(np.float32), BF16)
    return pl.pallas_call(
        functools.partial(_fox_cum_kernel, n_blocks=seq // Q_BLOCK),
        grid=(batch,),
        in_specs=[pl.BlockSpec((None, seq, LANES), lambda b: (b, 0, 0)),
                  _const_spec((1, LANES)), _const_spec((Q_BLOCK, Q_BLOCK))],
        out_specs=pl.BlockSpec((None, N_HEADS, seq), lambda b: (b, 0, 0)),
        out_shape=jax.ShapeDtypeStruct((batch, N_HEADS, seq), F32),
        compiler_params=_params(1),
        name="fox_cum",
    )(flog, b_pad, tri)


def _fox_kernel(q_ref, k_ref, v_ref, c_ref, o_ref, ks_ref, vs_ref, sa_ref, sb_ref, acc_ref, l_ref, m_ref,
                *, n_blocks):
    sel_k = _head_masks(FOX_K)
    k_per_q = FOX_Q // FOX_K

    def stack_heads(j, _):
        rows = pl.ds(pl.multiple_of(j * FOX_K, FOX_K), FOX_K)
        kj, vj = k_ref[rows, :], v_ref[rows, :]
        ks_ref[j] = jnp.concatenate([kj * sel_k[0], kj * sel_k[1]], axis=0)
        vs_ref[j] = jnp.concatenate(
            [jnp.concatenate([vj * sel_k[0], sel_k[0]], axis=1),
             jnp.concatenate([vj * sel_k[1], sel_k[1]], axis=1)], axis=0)
        return 0

    lax.fori_loop(0, n_blocks * k_per_q, stack_heads, 0)

    def scores(q_off, j, r0):
        q = q_ref[pl.ds(q_off + r0, FOX_Q - r0), :] * jnp.asarray(SCALE, BF16)
        return _dot_t(q, ks_ref[j])

    def consume(s2, q_off, j, r0, masked):
        rows = slice(r0, FOX_Q)
        n_rows = FOX_Q - r0
        head0 = lax.broadcasted_iota(jnp.int32, (n_rows, LANES), 1) < HEAD_DIM
        k_off = pl.multiple_of(j * FOX_K, FOX_K)
        ps, alphas = [], []
        for h in range(HEADS_PER_TILE):
            cj = c_ref[h, :, pl.ds(k_off, FOX_K)]
            s = s2[:, h * FOX_K:(h + 1) * FOX_K] - cj
            if masked:
                col_minus_row = (lax.broadcasted_iota(jnp.int32, (n_rows, FOX_K), 1)
                                 - lax.broadcasted_iota(jnp.int32, (n_rows, FOX_K), 0))
                s = jnp.where(col_minus_row <= q_off + r0 - k_off, s, NEG_INF)
            m_old = m_ref[h, rows]
            m_blk = jnp.max(jnp.maximum(s[:, :LANES], s[:, LANES:]), axis=-1, keepdims=True)
            m_new = jnp.maximum(m_old, jnp.broadcast_to(m_blk, (n_rows, LANES)))
            m_ref[h, rows] = m_new
            alphas.append(jnp.exp(m_old - m_new))
            ps.append(jnp.concatenate([jnp.exp(s[:, :LANES] - m_new),
                                       jnp.exp(s[:, LANES:] - m_new)], axis=1).astype(BF16))
        alpha = jnp.where(head0, alphas[0], alphas[1])
        pv = _dot(jnp.concatenate(ps, axis=1), vs_ref[j])
        acc_ref[rows] = acc_ref[rows] * alpha + pv[:, :LANES]
        l_ref[rows] = l_ref[rows] * alpha + pv[:, LANES:]

    def q_block(i, _):
        q_off = pl.multiple_of(i * FOX_Q, FOX_Q)
        acc_ref[...] = jnp.zeros_like(acc_ref)
        l_ref[...] = jnp.zeros_like(l_ref)
        m_ref[...] = jnp.full_like(m_ref, NEG_INF)
        sa_ref[...] = scores(q_off, 0, 0)

        def two_blocks(p, _):
            j = 2 * p
            s2 = sa_ref[...]
            sb_ref[...] = scores(q_off, j + 1, 0)
            consume(s2, q_off, j, 0, False)
            s2 = sb_ref[...]
            sa_ref[...] = scores(q_off, j + 2, 0)
            consume(s2, q_off, j + 1, 0, False)
            return 0

        lax.fori_loop(0, i, two_blocks, 0)
        j_diag = i * k_per_q
        consume(sa_ref[...], q_off, j_diag, 0, True)
        for d in range(1, k_per_q):
            consume(scores(q_off, j_diag + d, d * FOX_K), q_off, j_diag + d, d * FOX_K, True)
        o_ref[pl.ds(q_off, FOX_Q), :] = (acc_ref[...] / l_ref[...]).astype(BF16)
        return 0

    lax.fori_loop(0, n_blocks, q_block, 0)


def _fox_attention(qkv, cum, batch, seq):
    blk = lambda off: pl.BlockSpec((None, seq, LANES), lambda b, t: (b, 0, off + t))
    return pl.pallas_call(
        functools.partial(_fox_kernel, n_blocks=seq // FOX_Q),
        grid=(batch, N_HEAD_TILES),
        in_specs=[blk(0), blk(N_HEAD_TILES), blk(2 * N_HEAD_TILES),
                  pl.BlockSpec((None, HEADS_PER_TILE, 1, seq), lambda b, t: (b, t, 0, 0))],
        out_specs=pl.BlockSpec((None, seq, LANES), lambda b, t: (b, 0, t)),
        out_shape=jax.ShapeDtypeStruct((batch, seq, ATTN_WIDTH), BF16),
        scratch_shapes=[pltpu.VMEM((seq // FOX_K, 2 * FOX_K, LANES), BF16),
                        pltpu.VMEM((seq // FOX_K, 2 * FOX_K, 2 * LANES), BF16),
                        pltpu.VMEM((FOX_Q, 2 * FOX_K), F32),
                        pltpu.VMEM((FOX_Q, 2 * FOX_K), F32),
                        pltpu.VMEM((FOX_Q, LANES), F32),
                        pltpu.VMEM((FOX_Q, LANES), F32),
                        pltpu.VMEM((HEADS_PER_TILE, FOX_Q, LANES), F32)],
        compiler_params=_params(2),
        name="fox_attention",
    )(qkv, qkv, qkv, cum)


def _swa_kernel(sink_ref, q_ref, kp_ref, kc_ref, vp_ref, vc_ref, o_ref):
    g = pl.program_id(1)
    n = pl.program_id(2)
    window = 2 * Q_BLOCK
    lane = lax.broadcasted_iota(jnp.int32, (window, LANES), 1)
    kw = jnp.concatenate([kp_ref[...], kc_ref[...]], axis=0).astype(F32)
    vw = jnp.concatenate([vp_ref[...], vc_ref[...]], axis=0).astype(F32)
    mine = (lane >= g * HEAD_DIM) & (lane < (g + 1) * HEAD_DIM)
    kg = jnp.where(mine, kw, 0.0)
    vg = jnp.where(mine, vw, 0.0)
    k2 = (kg + pltpu.roll(kg, HEAD_DIM, 1)).astype(BF16)
    v2 = (vg + pltpu.roll(vg, HEAD_DIM, 1)).astype(BF16)
    sel = _head_masks(window)
    kstack = jnp.concatenate([k2 * sel[0], k2 * sel[1]], axis=0)
    vext = jnp.concatenate([jnp.concatenate([v2 * sel[0], sel[0]], axis=1),
                            jnp.concatenate([v2 * sel[1], sel[1]], axis=1)], axis=0)

    a = lax.broadcasted_iota(jnp.int32, (Q_BLOCK, window), 0)
    c = lax.broadcasted_iota(jnp.int32, (Q_BLOCK, window), 1)
    mask = (c > a) & (c <= a + SWA_WINDOW) & (c + n * Q_BLOCK >= Q_BLOCK)
    head0 = lax.broadcasted_iota(jnp.int32, (Q_BLOCK, LANES), 1) < HEAD_DIM

    for t in range(SWA_GROUP // HEADS_PER_TILE):
        qt = q_ref[:, t * LANES:(t + 1) * LANES] * jnp.asarray(SCALE, BF16)
        s2 = _dot_t(qt, kstack)
        es, sink_terms = [], []
        for h in range(HEADS_PER_TILE):
            s = jnp.where(mask, s2[:, h * window:(h + 1) * window], NEG_INF)
            sink = sink_ref[g * SWA_GROUP + t * HEADS_PER_TILE + h]
            m = jnp.max(jnp.maximum(s[:, :LANES], s[:, LANES:]), axis=-1, keepdims=True)
            m = jnp.maximum(jnp.broadcast_to(m, (Q_BLOCK, LANES)), sink)
            es.append(jnp.concatenate([jnp.exp(s[:, :LANES] - m),
                                       jnp.exp(s[:, LANES:] - m)], axis=1).astype(BF16))
            sink_terms.append(jnp.exp(sink - m))
        pv = _dot(jnp.concatenate(es, axis=1), vext)
        den = pv[:, LANES:] + jnp.where(head0, sink_terms[0], sink_terms[1])
        o_ref[:, t * LANES:(t + 1) * LANES] = (pv[:, :LANES] / den).astype(BF16)


def _swa_attention(qkv, sinks, batch, seq):
    nb = seq // Q_BLOCK
    gw = SWA_GROUP * HEAD_DIM
    k_tile = ATTN_WIDTH // LANES
    v_tile = k_tile + 1
    prev = lambda tile: pl.BlockSpec((None, Q_BLOCK, LANES),
                                     lambda b, g, n, s: (b, jnp.maximum(n - 1, 0), tile))
    cur = lambda tile: pl.BlockSpec((None, Q_BLOCK, LANES), lambda b, g, n, s: (b, n, tile))
    grid_spec = pltpu.PrefetchScalarGridSpec(
        num_scalar_prefetch=1,
        grid=(batch, SWA_KV_HEADS, nb),
        in_specs=[pl.BlockSpec((None, Q_BLOCK, gw), lambda b, g, n, s: (b, n, g)),
                  prev(k_tile), cur(k_tile), prev(v_tile), cur(v_tile)],
        out_specs=pl.BlockSpec((None, Q_BLOCK, gw), lambda b, g, n, s: (b, n, g)),
    )
    return pl.pallas_call(
        _swa_kernel,
        grid_spec=grid_spec,
        out_shape=jax.ShapeDtypeStruct((batch, seq, ATTN_WIDTH), BF16),
        compiler_params=_params(3),
        name="swa_attention",
    )(sinks, qkv, qkv, qkv, qkv, qkv)


def _tail_kernel(x_ref, a_ref, p_ref, wo_ref, mg_ref, wu_ref, wd_ref, pg_ref, wg_ref, wp_ref,
                 fg_ref, o_ref, h_ref, *, final):
    x1 = x_ref[...] + _dot(a_ref[...], wo_ref[...])
    h_ref[...] = _rms(x1, mg_ref[...]).astype(BF16)
    x2 = x1
    for c in range(0, D_FF, FF_CHUNK):
        u = jnp.maximum(_dot(h_ref[...], wu_ref[:, c:c + FF_CHUNK]), 0.0)
        x2 = x2 + _dot((u * u).astype(BF16), wd_ref[c:c + FF_CHUNK, :])
    h3 = _rms(x2, pg_ref[...]).astype(BF16)
    gate = 1.0 / (1.0 + jnp.exp(-_dot(h3, wg_ref[...])))
    x3 = x2 + _dot(p_ref[...].astype(BF16), wp_ref[...]) * gate
    o_ref[...] = _rms(x3, fg_ref[...]) if final else x3


def _tail(x2d, a2d, p3d, layer, wo, mg, wu, wd, pg, wg, wp, fg, final):
    t, d = x2d.shape
    tok = lambda w: pl.BlockSpec((TOKEN_TILE, w), lambda i: (i, 0))
    return pl.pallas_call(
        functools.partial(_tail_kernel, final=final),
        grid=(t // TOKEN_TILE,),
        in_specs=[tok(d), tok(ATTN_WIDTH),
                  pl.BlockSpec((None, TOKEN_TILE, PLE_DIM), lambda i: (layer, i, 0)),
                  _const_spec((ATTN_WIDTH, d)), _const_spec((1, d)),
                  _const_spec((d, D_FF)), _const_spec((D_FF, d)), _const_spec((1, d)),
                  _const_spec((d, d)), _const_spec((PLE_DIM, d)), _const_spec((1, d))],
        out_specs=tok(d),
        out_shape=jax.ShapeDtypeStruct((t, d), F32),
        scratch_shapes=[pltpu.VMEM((TOKEN_TILE, d), BF16)],
        compiler_params=_params(1),
        name="layer_tail",
    )(x2d, a2d, p3d, wo, mg, wu, wd, pg, wg, wp, fg)


def kernel(x, p, attn_norm_0, w_in_0, w_out_0, mlp_norm_0, w_up_0, w_down_0, ple_norm_0, w_ple_gate_0, w_ple_proj_0, attn_norm_1, w_in_1, w_out_1, sinks_1, mlp_norm_1, w_up_1, w_down_1, ple_norm_1, w_ple_gate_1, w_ple_proj_1, attn_norm_2, w_in_2, w_out_2, b_forget_2, mlp_norm_2, w_up_2, w_down_2, ple_norm_2, w_ple_gate_2, w_ple_proj_2, attn_norm_3, w_in_3, w_out_3, mlp_norm_3, w_up_3, w_down_3, ple_norm_3, w_ple_gate_3, w_ple_proj_3, final_norm):
    batch, seq, d = x.shape
    t = batch * seq
    layers = [
        (attn_norm_0, w_in_0, w_out_0, None, mlp_norm_0, w_up_0, w_down_0, ple_norm_0, w_ple_gate_0, w_ple_proj_0),
        (attn_norm_1, w_in_1, w_out_1, sinks_1, mlp_norm_1, w_up_1, w_down_1, ple_norm_1, w_ple_gate_1, w_ple_proj_1),
        (attn_norm_2, w_in_2, w_out_2, b_forget_2, mlp_norm_2, w_up_2, w_down_2, ple_norm_2, w_ple_gate_2, w_ple_proj_2),
        (attn_norm_3, w_in_3, w_out_3, None, mlp_norm_3, w_up_3, w_down_3, ple_norm_3, w_ple_gate_3, w_ple_proj_3),
    ]
    row = lambda g: g.reshape(1, d)
    bf = lambda w: w.astype(BF16)
    x2d = x.reshape(t, d)
    p3d = p.reshape(p.shape[0], t, PLE_DIM)
    fg = row(final_norm)
    for i, (an, wi, wo, extra, mn, wu, wd, pn, wg, wp) in enumerate(layers):
        kind = i % 3
        if kind == 0:
            qkv = _inproj(x2d, row(an), bf(wi))
            a = _sb_attention(qkv.reshape(batch, seq, -1), batch, seq)
        elif kind == 1:
            cos, s1, s2 = _rope_tables(seq)
            qkv = _inproj_swa(x2d, row(an), bf(wi), cos, s1, s2, seq)
            a = _swa_attention(qkv.reshape(batch, seq, -1), extra.astype(F32), batch, seq)
        else:
            wf = jnp.pad(wi[:, 3 * ATTN_WIDTH:], ((0, 0), (0, LANES - N_HEADS)))
            qkv, flog = _inproj_fox(x2d, row(an), bf(wi), bf(wf))
            b_pad = jnp.pad(extra.astype(F32), (0, LANES - N_HEADS)).reshape(1, LANES)
            cum = _fox_cum(flog.reshape(batch, seq, LANES), b_pad, batch, seq)
            a = _fox_attention(qkv.reshape(batch, seq, -1),
                               cum.reshape(batch, N_HEADS, 1, seq), batch, seq)
        x2d = _tail(x2d, a.reshape(t, ATTN_WIDTH), p3d, i,
                    bf(wo), row(mn), bf(wu), bf(wd), row(pn), bf(wg), bf(wp), fg,
                    final=(i == len(layers) - 1))
    return x2d.reshape(batch, seq, d)
```

```python
import functools

import jax
import jax.numpy as jnp
import numpy as np
from jax import lax
from jax.experimental import pallas as pl
from jax.experimental.pallas import tpu as pltpu

D_MODEL = 1024
HEAD_DIM = 64
N_HEADS = 16
ATTN_WIDTH = N_HEADS * HEAD_DIM
SWA_KV_HEADS = 2
SWA_GROUP = N_HEADS // SWA_KV_HEADS
SWA_WINDOW = 128
ROPE_THETA = 500000.0
ROPE_DIM = 16
D_FF = 4 * D_MODEL
PLE_DIM = 256
Q_BLOCK = 128
RMS_EPS = 1e-6
NEG_INF = -1e30
SCALE = HEAD_DIM ** -0.5

LANES = 128
HEADS_PER_TILE = LANES // HEAD_DIM
N_HEAD_TILES = ATTN_WIDTH // LANES
TOKEN_TILE = 512
FF_CHUNK = 1024
VMEM_LIMIT = 56 * 1024 * 1024

SB_Q = 2 * Q_BLOCK
SB_EXTRA = 2
SB_EXIT = 104.0
LOG2E = 1.4426950408889634
FOX_Q = 512
FOX_K = 256

assert FOX_K == 2 * LANES and FOX_Q == 2 * FOX_K
assert SB_Q == 2 * Q_BLOCK and SB_EXTRA <= SB_Q // Q_BLOCK

F32 = jnp.float32
BF16 = jnp.bfloat16


def _params(n_axes):
    return pltpu.CompilerParams(dimension_semantics=("parallel",) * n_axes,
                                vmem_limit_bytes=VMEM_LIMIT)


def _const_spec(shape):
    nd = len(shape)
    return pl.BlockSpec(shape, lambda *_: (0,) * nd, pipeline_mode=pl.Buffered(1))


def _rms(x, g):
    ms = jnp.mean(x * x, axis=-1, keepdims=True)
    return (x * lax.rsqrt(ms + RMS_EPS)) * g


def _dot(a, b):
    return jnp.dot(a, b, preferred_element_type=F32)


def _dot_t(a, b):
    return lax.dot_general(a, b, (((1,), (1,)), ((), ())), preferred_element_type=F32)


def _head_masks(rows=Q_BLOCK):
    lane = lax.broadcasted_iota(jnp.int32, (rows, LANES), 1)
    m0 = jnp.where(lane < HEAD_DIM, 1.0, 0.0).astype(BF16)
    m1 = jnp.where(lane < HEAD_DIM, 0.0, 1.0).astype(BF16)
    return m0, m1


def _inproj_kernel(x_ref, g_ref, w_ref, o_ref, h_ref, *, n_out):
    h_ref[...] = _rms(x_ref[...], g_ref[...]).astype(BF16)
    for c in range(0, n_out, FF_CHUNK):
        o_ref[:, c:c + FF_CHUNK] = _dot(h_ref[...], w_ref[:, c:c + FF_CHUNK].astype(BF16)).astype(BF16)


def _inproj(x2d, g, w):
    t, d = x2d.shape
    n_out = w.shape[1]
    return pl.pallas_call(
        functools.partial(_inproj_kernel, n_out=n_out),
        grid=(t // TOKEN_TILE,),
        in_specs=[pl.BlockSpec((TOKEN_TILE, d), lambda i: (i, 0)),
                  _const_spec((1, d)), _const_spec((d, n_out))],
        out_specs=pl.BlockSpec((TOKEN_TILE, n_out), lambda i: (i, 0)),
        out_shape=jax.ShapeDtypeStruct((t, n_out), BF16),
        scratch_shapes=[pltpu.VMEM((TOKEN_TILE, d), BF16)],
        compiler_params=_params(1),
        name="inproj",
    )(x2d, g, w)


def _inproj_fox_kernel(x_ref, g_ref, w_ref, wf_ref, o_ref, f_ref, h_ref, *, n_out):
    h_ref[...] = _rms(x_ref[...], g_ref[...]).astype(BF16)
    for c in range(0, n_out, FF_CHUNK):
        o_ref[:, c:c + FF_CHUNK] = _dot(h_ref[...], w_ref[:, c:c + FF_CHUNK].astype(BF16)).astype(BF16)
    f_ref[...] = _dot(h_ref[...], wf_ref[...])


def _inproj_fox(x2d, g, w, wf):
    t, d = x2d.shape
    n_out = 3 * ATTN_WIDTH
    return pl.pallas_call(
        functools.partial(_inproj_fox_kernel, n_out=n_out),
        grid=(t // TOKEN_TILE,),
        in_specs=[pl.BlockSpec((TOKEN_TILE, d), lambda i: (i, 0)),
                  _const_spec((1, d)), _const_spec(w.shape), _const_spec((d, LANES))],
        out_specs=[pl.BlockSpec((TOKEN_TILE, n_out), lambda i: (i, 0)),
                   pl.BlockSpec((TOKEN_TILE, LANES), lambda i: (i, 0))],
        out_shape=[jax.ShapeDtypeStruct((t, n_out), BF16),
                   jax.ShapeDtypeStruct((t, LANES), F32)],
        scratch_shapes=[pltpu.VMEM((TOKEN_TILE, d), BF16)],
        compiler_params=_params(1),
        name="inproj_fox",
    )(x2d, g, w, wf)


def _inproj_swa_kernel(x_ref, g_ref, w_ref, c_ref, s1_ref, s2_ref, o_ref, h_ref, *, n_rope, n_out):
    h_ref[...] = _rms(x_ref[...], g_ref[...]).astype(BF16)
    cos, s1, s2 = c_ref[...], s1_ref[...], s2_ref[...]
    for c in range(0, n_out, LANES):
        y = _dot(h_ref[...], w_ref[:, c:c + LANES].astype(BF16))
        if c < n_rope:
            y = y * cos + pltpu.roll(y, 8, 1) * s1 + pltpu.roll(y, LANES - 8, 1) * s2
        o_ref[:, c:c + LANES] = y.astype(BF16)


def _inproj_swa(x2d, g, w, cos, s1, s2, seq):
    t, d = x2d.shape
    n_out = w.shape[1]
    n_rope = ATTN_WIDTH + SWA_KV_HEADS * HEAD_DIM
    tiles_per_seq = seq // TOKEN_TILE
    tab = pl.BlockSpec((TOKEN_TILE, LANES), lambda i: (i % tiles_per_seq, 0))
    return pl.pallas_call(
        functools.partial(_inproj_swa_kernel, n_rope=n_rope, n_out=n_out),
        grid=(t // TOKEN_TILE,),
        in_specs=[pl.BlockSpec((TOKEN_TILE, d), lambda i: (i, 0)),
                  _const_spec((1, d)), _const_spec((d, n_out)), tab, tab, tab],
        out_specs=pl.BlockSpec((TOKEN_TILE, n_out), lambda i: (i, 0)),
        out_shape=jax.ShapeDtypeStruct((t, n_out), BF16),
        scratch_shapes=[pltpu.VMEM((TOKEN_TILE, d), BF16)],
        compiler_params=_params(1),
        name="inproj_swa",
    )(x2d, g, w, cos, s1, s2)


def _rope_tables(seq):
    half = ROPE_DIM // 2
    inv_freq = ROPE_THETA ** (-jnp.arange(half, dtype=F32) / half)
    ang = jnp.arange(seq, dtype=jnp.int32).astype(F32)[:, None] * inv_freq[None, :]
    cos, sin = jnp.cos(ang), jnp.sin(ang)
    one = jnp.ones((seq, HEAD_DIM - ROPE_DIM), F32)
    zero = jnp.zeros((seq, HEAD_DIM - ROPE_DIM), F32)
    zh = jnp.zeros((seq, half), F32)
    c_head = jnp.concatenate([cos, cos, one], axis=1)
    s1_head = jnp.concatenate([zh, sin, zero], axis=1)
    s2_head = jnp.concatenate([-sin, zh, zero], axis=1)
    tile = lambda a: jnp.concatenate([a] * HEADS_PER_TILE, axis=1)
    return tile(c_head), tile(s1_head), tile(s2_head)


def _sb_kernel(q_ref, k_ref, v_ref, uu_ref, o_ref, ks_ref, vs_ref, za_ref, zb_ref, acc_ref, r_ref,
               *, n_blocks):
    sel = _head_masks()
    blocks_per_q = SB_Q // Q_BLOCK
    n_static = blocks_per_q + SB_EXTRA
    scale = jnp.asarray(SCALE, BF16)

    def stack_heads(j, _):
        rows = pl.ds(pl.multiple_of(j * Q_BLOCK, Q_BLOCK), Q_BLOCK)
        kj, vj = k_ref[rows, :], v_ref[rows, :]
        ks_ref[j] = jnp.concatenate([kj * sel[0], kj * sel[1]], axis=0)
        vs_ref[j] = jnp.concatenate([vj * sel[0], vj * sel[1]], axis=0)
        return 0

    lax.fori_loop(0, n_blocks * blocks_per_q, stack_heads, 0)

    def strict(n_rows):
        return (lax.broadcasted_iota(jnp.int32, (n_rows, Q_BLOCK), 1)
                < lax.broadcasted_iota(jnp.int32, (n_rows, Q_BLOCK), 0))

    def scores_ahead(i):
        q = q_ref[pl.ds(pl.multiple_of(i * SB_Q, SB_Q), SB_Q), :] * scale
        ks = ks_ref[pl.ds(i * blocks_per_q - SB_EXTRA, n_static)]
        return _dot_t(q, ks.reshape(n_static * 2 * Q_BLOCK, LANES))

    def key_block(z2, j, r_in, keep):
        ws, r_out = [], []
        for h in range(HEADS_PER_TILE):
            z = z2[:, h * Q_BLOCK:(h + 1) * Q_BLOCK]
            sp = jnp.maximum(z, 0.0) + jnp.log(1.0 + jnp.exp2(jnp.abs(z) * -LOG2E))
            spm = sp if keep is None else jnp.where(keep, sp, 0.0)
            hi = spm.astype(BF16)
            lo = (spm - hi.astype(F32)).astype(BF16)
            cs = _dot(jnp.concatenate([hi, lo], axis=1), uu_ref[...])
            tail, tot = cs[:, :Q_BLOCK], cs[:, Q_BLOCK:]
            if r_in is not None:
                tail, tot = tail + r_in[h], tot + r_in[h]
            w = jnp.exp2((z - tail) * LOG2E)
            if keep is not None:
                w = jnp.where(keep, w, 0.0)
            ws.append(w.astype(BF16))
            r_out.append(tot)
        return _dot(jnp.concatenate(ws, axis=1), vs_ref[j]), r_out

    def q_block(i, n_extra, z_ref):
        static_i = isinstance(i, int)
        q_off = i * SB_Q if static_i else pl.multiple_of(i * SB_Q, SB_Q)
        j_diag = i * blocks_per_q
        q = q_ref[pl.ds(q_off, SB_Q), :] * scale

        def z_of(d, r0=0):
            if z_ref is None:
                return _dot_t(q_ref[pl.ds(q_off + r0, SB_Q - r0), :] * scale, ks_ref[j_diag - d])
            c0 = (SB_EXTRA - d) * 2 * Q_BLOCK
            return z_ref[r0:, c0:c0 + 2 * Q_BLOCK]

        zeros = jnp.zeros((Q_BLOCK, LANES), F32)
        pv_lo, r_lo = key_block(z_of(-1, Q_BLOCK), j_diag + 1, None, strict(Q_BLOCK))
        r = [jnp.concatenate([zeros, r_lo[h]], axis=0) for h in range(HEADS_PER_TILE)]
        pv, r = key_block(z_of(0), j_diag, r, strict(SB_Q))
        acc = pv + jnp.concatenate([zeros, pv_lo], axis=0)
        for d in range(1, n_extra + 1):
            pv, r = key_block(z_of(d), j_diag - d, r, None)
            acc = acc + pv
        if static_i and j_diag <= n_extra:
            o_ref[pl.ds(q_off, SB_Q), :] = acc.astype(BF16)
            return
        acc_ref[...] = acc
        for h in range(HEADS_PER_TILE):
            r_ref[h] = r[h]

        def more(c):
            j, r_min = c
            return (j >= 0) & (r_min <= SB_EXIT)

        def step(c):
            j, _ = c
            pv, r_new = key_block(_dot_t(q, ks_ref[j]), j,
                                  [r_ref[h] for h in range(HEADS_PER_TILE)], None)
            acc_ref[...] += pv
            for h in range(HEADS_PER_TILE):
                r_ref[h] = r_new[h]
            return j - 1, jnp.min(jnp.minimum(r_new[0], r_new[1]))

        lax.while_loop(more, step, (j_diag - n_extra - 1, jnp.min(jnp.minimum(r[0], r[1]))))
        o_ref[pl.ds(q_off, SB_Q), :] = acc_ref[...].astype(BF16)

    q_block(0, 0, None)
    za_ref[...] = scores_ahead(1)

    def two_blocks(p, _):
        i = 2 * p + 1
        zb_ref[...] = scores_ahead(i + 1)
        q_block(i, SB_EXTRA, za_ref)
        za_ref[...] = scores_ahead(i + 2)
        q_block(i + 1, SB_EXTRA, zb_ref)
        return 0

    lax.fori_loop(0, (n_blocks - 2) // 2, two_blocks, 0)
    q_block(n_blocks - 1, SB_EXTRA, za_ref)


def _cumsum_weights():
    a = np.arange(Q_BLOCK)
    u = (a[:, None] >= a[None, :]).astype(np.float32)
    half = np.concatenate([u, np.ones_like(u)], axis=1)
    return jnp.asarray(np.concatenate([half, half], axis=0), BF16)


def _sb_attention(qkv, batch, seq):
    assert seq % (2 * SB_Q) == 0, "query blocks after the first are processed in pairs plus one"
    blk = lambda off: pl.BlockSpec((None, seq, LANES), lambda b, t: (b, 0, off + t))
    return pl.pallas_call(
        functools.partial(_sb_kernel, n_blocks=seq // SB_Q),
        grid=(batch, N_HEAD_TILES),
        in_specs=[blk(0), blk(N_HEAD_TILES), blk(2 * N_HEAD_TILES),
                  _const_spec((2 * Q_BLOCK, 2 * Q_BLOCK))],
        out_specs=pl.BlockSpec((None, seq, LANES), lambda b, t: (b, 0, t)),
        out_shape=jax.ShapeDtypeStruct((batch, seq, ATTN_WIDTH), BF16),
        scratch_shapes=[pltpu.VMEM((seq // Q_BLOCK, 2 * Q_BLOCK, LANES), BF16),
                        pltpu.VMEM((seq // Q_BLOCK, 2 * Q_BLOCK, LANES), BF16),
                        pltpu.VMEM((SB_Q, (SB_Q // Q_BLOCK + SB_EXTRA) * 2 * Q_BLOCK), F32),
                        pltpu.VMEM((SB_Q, (SB_Q // Q_BLOCK + SB_EXTRA) * 2 * Q_BLOCK), F32),
                        pltpu.VMEM((SB_Q, LANES), F32),
                        pltpu.VMEM((HEADS_PER_TILE, SB_Q, Q_BLOCK), F32)],
        compiler_params=_params(2),
        name="sb_attention",
    )(qkv, qkv, qkv, _cumsum_weights())


def _fox_cum_kernel(f_ref, b_ref, tri_ref, o_ref, *, n_blocks):
    x = f_ref[...] + b_ref[...]
    lf = jnp.minimum(x, 0.0) - jnp.log1p(jnp.exp(-jnp.abs(x)))
    lft = lf.T
    carry = jnp.zeros((LANES, 1), F32)
    for c in range(n_blocks):
        blk = lft[:, c * Q_BLOCK:(c + 1) * Q_BLOCK]
        hi = blk.astype(BF16)
        r1 = blk - hi.astype(F32)
        mid = r1.astype(BF16)
        lo = (r1 - mid.astype(F32)).astype(BF16)
        cs = _dot(hi, tri_ref[...]) + _dot(mid, tri_ref[...]) + _dot(lo, tri_ref[...]) + carry
        o_ref[:, c * Q_BLOCK:(c + 1) * Q_BLOCK] = cs[:N_HEADS]
        carry = cs[:, Q_BLOCK - 1:Q_BLOCK]


def _fox_cum(flog, b_pad, batch, seq):
    a = np.arange(Q_BLOCK)
    tri = jnp.asarray((a[:, None] <= a[None, :]).astype(np.float32), BF16)
    return pl.pallas_call(
        functools.partial(_fox_cum_kernel, n_blocks=seq // Q_BLOCK),
        grid=(batch,),
        in_specs=[pl.BlockSpec((None, seq, LANES), lambda b: (b, 0, 0)),
                  _const_spec((1, LANES)), _const_spec((Q_BLOCK, Q_BLOCK))],
        out_specs=pl.BlockSpec((None, N_HEADS, seq), lambda b: (b, 0, 0)),
        out_shape=jax.ShapeDtypeStruct((batch, N_HEADS, seq), F32),
        compiler_params=_params(1),
        name="fox_cum",
    )(flog, b_pad, tri)


def _fox_kernel(q_ref, k_ref, v_ref, c_ref, o_ref, ks_ref, vs_ref, sa_ref, sb_ref, acc_ref, l_ref, m_ref,
                *, n_blocks):
    sel_k = _head_masks(FOX_K)
    k_per_q = FOX_Q // FOX_K

    def stack_heads(j, _):
        rows = pl.ds(pl.multiple_of(j * FOX_K, FOX_K), FOX_K)
        kj, vj = k_ref[rows, :], v_ref[rows, :]
        ks_ref[j] = jnp.concatenate([kj * sel_k[0], kj * sel_k[1]], axis=0)
        vs_ref[j] = jnp.concatenate(
            [jnp.concatenate([vj * sel_k[0], sel_k[0]], axis=1),
             jnp.concatenate([vj * sel_k[1], sel_k[1]], axis=1)], axis=0)
        return 0

    lax.fori_loop(0, n_blocks * k_per_q, stack_heads, 0)

    def scores(q_off, j, r0):
        q = q_ref[pl.ds(q_off + r0, FOX_Q - r0), :] * jnp.asarray(SCALE, BF16)
        return _dot_t(q, ks_ref[j])

    def consume(s2, q_off, j, r0, masked):
        rows = slice(r0, FOX_Q)
        n_rows = FOX_Q - r0
        head0 = lax.broadcasted_iota(jnp.int32, (n_rows, LANES), 1) < HEAD_DIM
        k_off = pl.multiple_of(j * FOX_K, FOX_K)
        ps, alphas = [], []
        for h in range(HEADS_PER_TILE):
            cj = c_ref[h, :, pl.ds(k_off, FOX_K)]
            s = s2[:, h * FOX_K:(h + 1) * FOX_K] - cj
            if masked:
                col_minus_row = (lax.broadcasted_iota(jnp.int32, (n_rows, FOX_K), 1)
                                 - lax.broadcasted_iota(jnp.int32, (n_rows, FOX_K), 0))
                s = jnp.where(col_minus_row <= q_off + r0 - k_off, s, NEG_INF)
            m_old = m_ref[h, rows]
            m_blk = jnp.max(jnp.maximum(s[:, :LANES], s[:, LANES:]), axis=-1, keepdims=True)
            m_new = jnp.maximum(m_old, jnp.broadcast_to(m_blk, (n_rows, LANES)))
            m_ref[h, rows] = m_new
            alphas.append(jnp.exp(m_old - m_new))
            ps.append(jnp.concatenate([jnp.exp(s[:, :LANES] - m_new),
                                       jnp.exp(s[:, LANES:] - m_new)], axis=1).astype(BF16))
        alpha = jnp.where(head0, alphas[0], alphas[1])
        pv = _dot(jnp.concatenate(ps, axis=1), vs_ref[j])
        acc_ref[rows] = acc_ref[rows] * alpha + pv[:, :LANES]
        l_ref[rows] = l_ref[rows] * alpha + pv[:, LANES:]

    def q_block(i, _):
        q_off = pl.multiple_of(i * FOX_Q, FOX_Q)
        acc_ref[...] = jnp.zeros_like(acc_ref)
        l_ref[...] = jnp.zeros_like(l_ref)
        m_ref[...] = jnp.full_like(m_ref, NEG_INF)
        sa_ref[...] = scores(q_off, 0, 0)

        def two_blocks(p, _):
            j = 2 * p
            s2 = sa_ref[...]
            sb_ref[...] = scores(q_off, j + 1, 0)
            consume(s2, q_off, j, 0, False)
            s2 = sb_ref[...]
            sa_ref[...] = scores(q_off, j + 2, 0)
            consume(s2, q_off, j + 1, 0, False)
            return 0

        lax.fori_loop(0, i, two_blocks, 0)
        j_diag = i * k_per_q
        consume(sa_ref[...], q_off, j_diag, 0, True)
        for d in range(1, k_per_q):
            consume(scores(q_off, j_diag + d, d * FOX_K), q_off, j_diag + d, d * FOX_K, True)
        o_ref[pl.ds(q_off, FOX_Q), :] = (acc_ref[...] / l_ref[...]).astype(BF16)
        return 0

    lax.fori_loop(0, n_blocks, q_block, 0)


def _fox_attention(qkv, cum, batch, seq):
    blk = lambda off: pl.BlockSpec((None, seq, LANES), lambda b, t: (b, 0, off + t))
    return pl.pallas_call(
        functools.partial(_fox_kernel, n_blocks=seq // FOX_Q),
        grid=(batch, N_HEAD_TILES),
        in_specs=[blk(0), blk(N_HEAD_TILES), blk(2 * N_HEAD_TILES),
                  pl.BlockSpec((None, HEADS_PER_TILE, 1, seq), lambda b, t: (b, t, 0, 0))],
        out_specs=pl.BlockSpec((None, seq, LANES), lambda b, t: (b, 0, t)),
        out_shape=jax.ShapeDtypeStruct((batch, seq, ATTN_WIDTH), BF16),
        scratch_shapes=[pltpu.VMEM((seq // FOX_K, 2 * FOX_K, LANES), BF16),
                        pltpu.VMEM((seq // FOX_K, 2 * FOX_K, 2 * LANES), BF16),
                        pltpu.VMEM((FOX_Q, 2 * FOX_K), F32),
                        pltpu.VMEM((FOX_Q, 2 * FOX_K), F32),
                        pltpu.VMEM((FOX_Q, LANES), F32),
                        pltpu.VMEM((FOX_Q, LANES), F32),
                        pltpu.VMEM((HEADS_PER_TILE, FOX_Q, LANES), F32)],
        compiler_params=_params(2),
        name="fox_attention",
    )(qkv, qkv, qkv, cum)


def _swa_kernel(sink_ref, q_ref, kp_ref, kc_ref, vp_ref, vc_ref, o_ref):
    g = pl.program_id(1)
    n = pl.program_id(2)
    window = 2 * Q_BLOCK
    lane = lax.broadcasted_iota(jnp.int32, (window, LANES), 1)
    kw = jnp.concatenate([kp_ref[...], kc_ref[...]], axis=0).astype(F32)
    vw = jnp.concatenate([vp_ref[...], vc_ref[...]], axis=0).astype(F32)
    mine = (lane >= g * HEAD_DIM) & (lane < (g + 1) * HEAD_DIM)
    kg = jnp.where(mine, kw, 0.0)
    vg = jnp.where(mine, vw, 0.0)
    k2 = (kg + pltpu.roll(kg, HEAD_DIM, 1)).astype(BF16)
    v2 = (vg + pltpu.roll(vg, HEAD_DIM, 1)).astype(BF16)
    sel = _head_masks(window)
    kstack = jnp.concatenate([k2 * sel[0], k2 * sel[1]], axis=0)
    vext = jnp.concatenate([jnp.concatenate([v2 * sel[0], sel[0]], axis=1),
                            jnp.concatenate([v2 * sel[1], sel[1]], axis=1)], axis=0)

    a = lax.broadcasted_iota(jnp.int32, (Q_BLOCK, window), 0)
    c = lax.broadcasted_iota(jnp.int32, (Q_BLOCK, window), 1)
    mask = (c > a) & (c <= a + SWA_WINDOW) & (c + n * Q_BLOCK >= Q_BLOCK)
    head0 = lax.broadcasted_iota(jnp.int32, (Q_BLOCK, LANES), 1) < HEAD_DIM

    for t in range(SWA_GROUP // HEADS_PER_TILE):
        qt = q_ref[:, t * LANES:(t + 1) * LANES] * jnp.asarray(SCALE, BF16)
        s2 = _dot_t(qt, kstack)
        es, sink_terms = [], []
        for h in range(HEADS_PER_TILE):
            s = jnp.where(mask, s2[:, h * window:(h + 1) * window], NEG_INF)
            sink = sink_ref[g * SWA_GROUP + t * HEADS_PER_TILE + h]
            m = jnp.max(jnp.maximum(s[:, :LANES], s[:, LANES:]), axis=-1, keepdims=True)
            m = jnp.maximum(jnp.broadcast_to(m, (Q_BLOCK, LANES)), sink)
            es.append(jnp.concatenate([jnp.exp(s[:, :LANES] - m),
                                       jnp.exp(s[:, LANES:] - m)], axis=1).astype(BF16))
            sink_terms.append(jnp.exp(sink - m))
        pv = _dot(jnp.concatenate(es, axis=1), vext)
        den = pv[:, LANES:] + jnp.where(head0, sink_terms[0], sink_terms[1])
        o_ref[:, t * LANES:(t + 1) * LANES] = (pv[:, :LANES] / den).astype(BF16)


def _swa_attention(qkv, sinks, batch, seq):
    nb = seq // Q_BLOCK
    gw = SWA_GROUP * HEAD_DIM
    k_tile = ATTN_WIDTH // LANES
    v_tile = k_tile + 1
    prev = lambda tile: pl.BlockSpec((None, Q_BLOCK, LANES),
                                     lambda b, g, n, s: (b, jnp.maximum(n - 1, 0), tile))
    cur = lambda tile: pl.BlockSpec((None, Q_BLOCK, LANES), lambda b, g, n, s: (b, n, tile))
    grid_spec = pltpu.PrefetchScalarGridSpec(
        num_scalar_prefetch=1,
        grid=(batch, SWA_KV_HEADS, nb),
        in_specs=[pl.BlockSpec((None, Q_BLOCK, gw), lambda b, g, n, s: (b, n, g)),
                  prev(k_tile), cur(k_tile), prev(v_tile), cur(v_tile)],
        out_specs=pl.BlockSpec((None, Q_BLOCK, gw), lambda b, g, n, s: (b, n, g)),
    )
    return pl.pallas_call(
        _swa_kernel,
        grid_spec=grid_spec,
        out_shape=jax.ShapeDtypeStruct((batch, seq, ATTN_WIDTH), BF16),
        compiler_params=_params(3),
        name="swa_attention",
    )(sinks, qkv, qkv, qkv, qkv, qkv)


def _tail_kernel(x_ref, a_ref, p_ref, wo_ref, mg_ref, wu_ref, wd_ref, pg_ref, wg_ref, wp_ref,
                 fg_ref, o_ref, h_ref, *, final):
    x1 = x_ref[...] + _dot(a_ref[...], wo_ref[...].astype(BF16))
    h_ref[...] = _rms(x1, mg_ref[...]).astype(BF16)
    x2 = x1
    for c in range(0, D_FF, FF_CHUNK):
        u = jnp.maximum(_dot(h_ref[...], wu_ref[:, c:c + FF_CHUNK].astype(BF16)), 0.0)
        x2 = x2 + _dot((u * u).astype(BF16), wd_ref[c:c + FF_CHUNK, :])
    h3 = _rms(x2, pg_ref[...]).astype(BF16)
    gate = 1.0 / (1.0 + jnp.exp(-_dot(h3, wg_ref[...].astype(BF16))))
    x3 = x2 + _dot(p_ref[...].astype(BF16), wp_ref[...].astype(BF16)) * gate
    o_ref[...] = _rms(x3, fg_ref[...]) if final else x3


def _tail(x2d, a2d, p3d, layer, wo, mg, wu, wd, pg, wg, wp, fg, final):
    t, d = x2d.shape
    tok = lambda w: pl.BlockSpec((TOKEN_TILE, w), lambda i: (i, 0))
    return pl.pallas_call(
        functools.partial(_tail_kernel, final=final),
        grid=(t // TOKEN_TILE,),
        in_specs=[tok(d), tok(ATTN_WIDTH),
                  pl.BlockSpec((None, TOKEN_TILE, PLE_DIM), lambda i: (layer, i, 0)),
                  _const_spec((ATTN_WIDTH, d)), _const_spec((1, d)),
                  _const_spec((d, D_FF)), _const_spec((D_FF, d)), _const_spec((1, d)),
                  _const_spec((d, d)), _const_spec((PLE_DIM, d)), _const_spec((1, d))],
        out_specs=tok(d),
        out_shape=jax.ShapeDtypeStruct((t, d), F32),
        scratch_shapes=[pltpu.VMEM((TOKEN_TILE, d), BF16)],
        compiler_params=_params(1),
        name="layer_tail",
    )(x2d, a2d, p3d, wo, mg, wu, wd, pg, wg, wp, fg)


def kernel(x, p, attn_norm_0, w_in_0, w_out_0, mlp_norm_0, w_up_0, w_down_0, ple_norm_0, w_ple_gate_0, w_ple_proj_0, attn_norm_1, w_in_1, w_out_1, sinks_1, mlp_norm_1, w_up_1, w_down_1, ple_norm_1, w_ple_gate_1, w_ple_proj_1, attn_norm_2, w_in_2, w_out_2, b_forget_2, mlp_norm_2, w_up_2, w_down_2, ple_norm_2, w_ple_gate_2, w_ple_proj_2, attn_norm_3, w_in_3, w_out_3, mlp_norm_3, w_up_3, w_down_3, ple_norm_3, w_ple_gate_3, w_ple_proj_3, final_norm):
    batch, seq, d = x.shape
    t = batch * seq
    layers = [
        (attn_norm_0, w_in_0, w_out_0, None, mlp_norm_0, w_up_0, w_down_0, ple_norm_0, w_ple_gate_0, w_ple_proj_0),
        (attn_norm_1, w_in_1, w_out_1, sinks_1, mlp_norm_1, w_up_1, w_down_1, ple_norm_1, w_ple_gate_1, w_ple_proj_1),
        (attn_norm_2, w_in_2, w_out_2, b_forget_2, mlp_norm_2, w_up_2, w_down_2, ple_norm_2, w_ple_gate_2, w_ple_proj_2),
        (attn_norm_3, w_in_3, w_out_3, None, mlp_norm_3, w_up_3, w_down_3, ple_norm_3, w_ple_gate_3, w_ple_proj_3),
    ]
    row = lambda g: g.reshape(1, d)
    bf = lambda w: w.astype(BF16)
    x2d = x.reshape(t, d)
    p3d = p.reshape(p.shape[0], t, PLE_DIM)
    fg = row(final_norm)
    for i, (an, wi, wo, extra, mn, wu, wd, pn, wg, wp) in enumerate(layers):
        kind = i % 3
        if kind == 0:
            qkv = _inproj(x2d, row(an), wi)
            a = _sb_attention(qkv.reshape(batch, seq, -1), batch, seq)
        elif kind == 1:
            cos, s1, s2 = _rope_tables(seq)
            qkv = _inproj_swa(x2d, row(an), wi, cos, s1, s2, seq)
            a = _swa_attention(qkv.reshape(batch, seq, -1), extra.astype(F32), batch, seq)
        else:
            wf = jnp.pad(wi[:, 3 * ATTN_WIDTH:], ((0, 0), (0, LANES - N_HEADS)))
            qkv, flog = _inproj_fox(x2d, row(an), wi, bf(wf))
            b_pad = jnp.pad(extra.astype(F32), (0, LANES - N_HEADS)).reshape(1, LANES)
            cum = _fox_cum(flog.reshape(batch, seq, LANES), b_pad, batch, seq)
            a = _fox_attention(qkv.reshape(batch, seq, -1),
                               cum.reshape(batch, N_HEADS, 1, seq), batch, seq)
        x2d = _tail(x2d, a.reshape(t, ATTN_WIDTH), p3d, i,
                    wo, row(mn), wu, bf(wd), row(pn), wg, wp, fg,
                    final=(i == len(layers) - 1))
    return x2d.reshape(batch, seq, d)
```

```python
import functools

import jax
import jax.numpy as jnp
import numpy as np
from jax import lax
from jax.experimental import pallas as pl
from jax.experimental.pallas import tpu as pltpu

D_MODEL = 1024
HEAD_DIM = 64
N_HEADS = 16
ATTN_WIDTH = N_HEADS * HEAD_DIM
SWA_KV_HEADS = 2
SWA_GROUP = N_HEADS // SWA_KV_HEADS
SWA_WINDOW = 128
ROPE_THETA = 500000.0
ROPE_DIM = 16
D_FF = 4 * D_MODEL
PLE_DIM = 256
Q_BLOCK = 128
RMS_EPS = 1e-6
NEG_INF = -1e30
SCALE = HEAD_DIM ** -0.5

LANES = 128
HEADS_PER_TILE = LANES // HEAD_DIM
N_HEAD_TILES = ATTN_WIDTH // LANES
TOKEN_TILE = 512
FF_CHUNK = 1024
VMEM_LIMIT = 56 * 1024 * 1024

SB_Q = 2 * Q_BLOCK
SB_EXTRA = 2
SB_EXIT = 104.0
LOG2E = 1.4426950408889634
FOX_Q = 512
FOX_K = 256

assert FOX_K == 2 * LANES and FOX_Q == 2 * FOX_K
assert SB_Q == 2 * Q_BLOCK and SB_EXTRA <= SB_Q // Q_BLOCK

F32 = jnp.float32
BF16 = jnp.bfloat16


def _params(n_axes):
    return pltpu.CompilerParams(dimension_semantics=("parallel",) * n_axes,
                                vmem_limit_bytes=VMEM_LIMIT)


def _const_spec(shape):
    nd = len(shape)
    return pl.BlockSpec(shape, lambda *_: (0,) * nd, pipeline_mode=pl.Buffered(1))


def _rms(x, g):
    ms = jnp.mean(x * x, axis=-1, keepdims=True)
    return (x * lax.rsqrt(ms + RMS_EPS)) * g


def _dot(a, b):
    return jnp.dot(a, b, preferred_element_type=F32)


def _dot_t(a, b):
    return lax.dot_general(a, b, (((1,), (1,)), ((), ())), preferred_element_type=F32)


def _head_masks(rows=Q_BLOCK):
    lane = lax.broadcasted_iota(jnp.int32, (rows, LANES), 1)
    m0 = jnp.where(lane < HEAD_DIM, 1.0, 0.0).astype(BF16)
    m1 = jnp.where(lane < HEAD_DIM, 0.0, 1.0).astype(BF16)
    return m0, m1


def _inproj_kernel(x_ref, g_ref, w_ref, o_ref, h_ref, *, n_out):
    h_ref[...] = _rms(x_ref[...], g_ref[...]).astype(BF16)
    for c in range(0, n_out, FF_CHUNK):
        o_ref[:, c:c + FF_CHUNK] = _dot(h_ref[...], w_ref[:, c:c + FF_CHUNK].astype(BF16)).astype(BF16)


def _inproj(x2d, g, w):
    t, d = x2d.shape
    n_out = w.shape[1]
    return pl.pallas_call(
        functools.partial(_inproj_kernel, n_out=n_out),
        grid=(t // TOKEN_TILE,),
        in_specs=[pl.BlockSpec((TOKEN_TILE, d), lambda i: (i, 0)),
                  _const_spec((1, d)), _const_spec((d, n_out))],
        out_specs=pl.BlockSpec((TOKEN_TILE, n_out), lambda i: (i, 0)),
        out_shape=jax.ShapeDtypeStruct((t, n_out), BF16),
        scratch_shapes=[pltpu.VMEM((TOKEN_TILE, d), BF16)],
        compiler_params=_params(1),
        name="inproj",
    )(x2d, g, w)


def _inproj_fox_kernel(x_ref, g_ref, w_ref, wf_ref, o_ref, f_ref, h_ref, *, n_out):
    h_ref[...] = _rms(x_ref[...], g_ref[...]).astype(BF16)
    for c in range(0, n_out, FF_CHUNK):
        o_ref[:, c:c + FF_CHUNK] = _dot(h_ref[...], w_ref[:, c:c + FF_CHUNK].astype(BF16)).astype(BF16)
    f_ref[...] = _dot(h_ref[...], wf_ref[...])


def _inproj_fox(x2d, g, w, wf):
    t, d = x2d.shape
    n_out = 3 * ATTN_WIDTH
    return pl.pallas_call(
        functools.partial(_inproj_fox_kernel, n_out=n_out),
        grid=(t // TOKEN_TILE,),
        in_specs=[pl.BlockSpec((TOKEN_TILE, d), lambda i: (i, 0)),
                  _const_spec((1, d)), _const_spec(w.shape), _const_spec((d, LANES))],
        out_specs=[pl.BlockSpec((TOKEN_TILE, n_out), lambda i: (i, 0)),
                   pl.BlockSpec((TOKEN_TILE, LANES), lambda i: (i, 0))],
        out_shape=[jax.ShapeDtypeStruct((t, n_out), BF16),
                   jax.ShapeDtypeStruct((t, LANES), F32)],
        scratch_shapes=[pltpu.VMEM((TOKEN_TILE, d), BF16)],
        compiler_params=_params(1),
        name="inproj_fox",
    )(x2d, g, w, wf)


def _inproj_swa_kernel(x_ref, g_ref, w_ref, c_ref, s1_ref, s2_ref, o_ref, h_ref, *, n_rope, n_out):
    h_ref[...] = _rms(x_ref[...], g_ref[...]).astype(BF16)
    cos, s1, s2 = c_ref[...], s1_ref[...], s2_ref[...]
    for c in range(0, n_out, LANES):
        y = _dot(h_ref[...], w_ref[:, c:c + LANES].astype(BF16))
        if c < n_rope:
            y = y * cos + pltpu.roll(y, 8, 1) * s1 + pltpu.roll(y, LANES - 8, 1) * s2
        o_ref[:, c:c + LANES] = y.astype(BF16)


def _inproj_swa(x2d, g, w, cos, s1, s2, seq):
    t, d = x2d.shape
    n_out = w.shape[1]
    n_rope = ATTN_WIDTH + SWA_KV_HEADS * HEAD_DIM
    tiles_per_seq = seq // TOKEN_TILE
    tab = pl.BlockSpec((TOKEN_TILE, LANES), lambda i: (i % tiles_per_seq, 0))
    return pl.pallas_call(
        functools.partial(_inproj_swa_kernel, n_rope=n_rope, n_out=n_out),
        grid=(t // TOKEN_TILE,),
        in_specs=[pl.BlockSpec((TOKEN_TILE, d), lambda i: (i, 0)),
                  _const_spec((1, d)), _const_spec((d, n_out)), tab, tab, tab],
        out_specs=pl.BlockSpec((TOKEN_TILE, n_out), lambda i: (i, 0)),
        out_shape=jax.ShapeDtypeStruct((t, n_out), BF16),
        scratch_shapes=[pltpu.VMEM((TOKEN_TILE, d), BF16)],
        compiler_params=_params(1),
        name="inproj_swa",
    )(x2d, g, w, cos, s1, s2)


def _rope_tables(seq):
    half = ROPE_DIM // 2
    inv_freq = ROPE_THETA ** (-jnp.arange(half, dtype=F32) / half)
    ang = jnp.arange(seq, dtype=jnp.int32).astype(F32)[:, None] * inv_freq[None, :]
    cos, sin = jnp.cos(ang), jnp.sin(ang)
    one = jnp.ones((seq, HEAD_DIM - ROPE_DIM), F32)
    zero = jnp.zeros((seq, HEAD_DIM - ROPE_DIM), F32)
    zh = jnp.zeros((seq, half), F32)
    c_head = jnp.concatenate([cos, cos, one], axis=1)
    s1_head = jnp.concatenate([zh, sin, zero], axis=1)
    s2_head = jnp.concatenate([-sin, zh, zero], axis=1)
    tile = lambda a: jnp.concatenate([a] * HEADS_PER_TILE, axis=1)
    return tile(c_head), tile(s1_head), tile(s2_head)


def _sb_kernel(q_ref, k_ref, v_ref, uu_ref, o_ref, ks_ref, vs_ref, za_ref, zb_ref, acc_ref, r_ref,
               rmin_ref, *, n_blocks):
    sel = _head_masks()
    blocks_per_q = SB_Q // Q_BLOCK
    n_static = blocks_per_q + SB_EXTRA
    scale = jnp.asarray(SCALE, BF16)

    def stack_heads(j, _):
        rows = pl.ds(pl.multiple_of(j * Q_BLOCK, Q_BLOCK), Q_BLOCK)
        kj, vj = k_ref[rows, :], v_ref[rows, :]
        ks_ref[j] = jnp.concatenate([kj * sel[0], kj * sel[1]], axis=0)
        vs_ref[j] = jnp.concatenate([vj * sel[0], vj * sel[1]], axis=0)
        return 0

    lax.fori_loop(0, n_blocks * blocks_per_q, stack_heads, 0)

    def strict(n_rows):
        return (lax.broadcasted_iota(jnp.int32, (n_rows, Q_BLOCK), 1)
                < lax.broadcasted_iota(jnp.int32, (n_rows, Q_BLOCK), 0))

    def scores_ahead(i):
        q = q_ref[pl.ds(i * SB_Q, SB_Q), :] * scale
        ks = ks_ref[pl.ds(i * blocks_per_q - SB_EXTRA, n_static)]
        return _dot_t(q, ks.reshape(n_static * 2 * Q_BLOCK, LANES))

    def key_block(z2, j, r_in, keep):
        ws, r_out = [], []
        for h in range(HEADS_PER_TILE):
            z = z2[:, h * Q_BLOCK:(h + 1) * Q_BLOCK]
            sp = jnp.maximum(z, 0.0) + jnp.log(1.0 + jnp.exp2(jnp.abs(z) * -LOG2E))
            spm = sp if keep is None else jnp.where(keep, sp, 0.0)
            hi = spm.astype(BF16)
            lo = (spm - hi.astype(F32)).astype(BF16)
            cs = _dot(jnp.concatenate([hi, lo], axis=1), uu_ref[...])
            tail, tot = cs[:, :Q_BLOCK], cs[:, Q_BLOCK:]
            if r_in is not None:
                tail, tot = tail + r_in[h], tot + r_in[h]
            w = jnp.exp2((z - tail) * LOG2E)
            if keep is not None:
                w = jnp.where(keep, w, 0.0)
            ws.append(w.astype(BF16))
            r_out.append(tot)
        return _dot(jnp.concatenate(ws, axis=1), vs_ref[j]), r_out

    def first_blocks(i, n_extra, z_ref):
        q_off = i * SB_Q
        j_diag = i * blocks_per_q

        def z_of(d, r0=0):
            if z_ref is None:
                return _dot_t(q_ref[pl.ds(q_off + r0, SB_Q - r0), :] * scale, ks_ref[j_diag - d])
            c0 = (SB_EXTRA - d) * 2 * Q_BLOCK
            return z_ref[r0:, c0:c0 + 2 * Q_BLOCK]

        zeros = jnp.zeros((Q_BLOCK, LANES), F32)
        pv_lo, r_lo = key_block(z_of(-1, Q_BLOCK), j_diag + 1, None, strict(Q_BLOCK))
        r = [jnp.concatenate([zeros, r_lo[h]], axis=0) for h in range(HEADS_PER_TILE)]
        pv, r = key_block(z_of(0), j_diag, r, strict(SB_Q))
        acc = pv + jnp.concatenate([zeros, pv_lo], axis=0)
        for d in range(1, n_extra + 1):
            pv, r = key_block(z_of(d), j_diag - d, r, None)
            acc = acc + pv
        acc_ref[pl.ds(q_off, SB_Q), :] = acc
        for h in range(HEADS_PER_TILE):
            r_ref[h, pl.ds(q_off, SB_Q), :] = r[h]
        rmin_ref[i] = jnp.min(jnp.minimum(r[0], r[1]))

    def remaining_blocks(i, _):
        rows = pl.ds(pl.multiple_of(i * SB_Q, SB_Q), SB_Q)
        q = q_ref[rows, :] * scale

        def more(c):
            j, r_min = c
            return (j >= 0) & (r_min <= SB_EXIT)

        def step(c):
            j, _ = c
            pv, r_new = key_block(_dot_t(q, ks_ref[j]), j,
                                  [r_ref[h, rows, :] for h in range(HEADS_PER_TILE)], None)
            acc_ref[rows, :] += pv
            for h in range(HEADS_PER_TILE):
                r_ref[h, rows, :] = r_new[h]
            return j - 1, jnp.min(jnp.minimum(r_new[0], r_new[1]))

        lax.while_loop(more, step, (i * blocks_per_q - SB_EXTRA - 1, rmin_ref[i]))
        o_ref[rows, :] = acc_ref[rows, :].astype(BF16)
        return 0

    z_refs = (za_ref, zb_ref)
    for i in range(n_blocks):
        if i + 1 < n_blocks:
            z_refs[(i + 1) % 2][...] = scores_ahead(i + 1)
        first_blocks(i, SB_EXTRA if i else 0, z_refs[i % 2] if i else None)
    o_ref[pl.ds(0, SB_Q), :] = acc_ref[pl.ds(0, SB_Q), :].astype(BF16)
    lax.fori_loop(1, n_blocks, remaining_blocks, 0)


def _cumsum_weights():
    a = np.arange(Q_BLOCK)
    u = (a[:, None] >= a[None, :]).astype(np.float32)
    half = np.concatenate([u, np.ones_like(u)], axis=1)
    return jnp.asarray(np.concatenate([half, half], axis=0), BF16)


def _sb_attention(qkv, batch, seq):
    blk = lambda off: pl.BlockSpec((None, seq, LANES), lambda b, t: (b, 0, off + t))
    return pl.pallas_call(
        functools.partial(_sb_kernel, n_blocks=seq // SB_Q),
        grid=(batch, N_HEAD_TILES),
        in_specs=[blk(0), blk(N_HEAD_TILES), blk(2 * N_HEAD_TILES),
                  _const_spec((2 * Q_BLOCK, 2 * Q_BLOCK))],
        out_specs=pl.BlockSpec((None, seq, LANES), lambda b, t: (b, 0, t)),
        out_shape=jax.ShapeDtypeStruct((batch, seq, ATTN_WIDTH), BF16),
        scratch_shapes=[pltpu.VMEM((seq // Q_BLOCK, 2 * Q_BLOCK, LANES), BF16),
                        pltpu.VMEM((seq // Q_BLOCK, 2 * Q_BLOCK, LANES), BF16),
                        pltpu.VMEM((SB_Q, (SB_Q // Q_BLOCK + SB_EXTRA) * 2 * Q_BLOCK), F32),
                        pltpu.VMEM((SB_Q, (SB_Q // Q_BLOCK + SB_EXTRA) * 2 * Q_BLOCK), F32),
                        pltpu.VMEM((seq, LANES), F32),
                        pltpu.VMEM((HEADS_PER_TILE, seq, Q_BLOCK), F32),
                        pltpu.SMEM((seq // SB_Q,), F32)],
        compiler_params=_params(2),
        name="sb_attention",
    )(qkv, qkv, qkv, _cumsum_weights())


def _fox_cum_kernel(f_ref, b_ref, tri_ref, o_ref, *, n_blocks):
    x = f_ref[...] + b_ref[...]
    lf = jnp.minimum(x, 0.0) - jnp.log1p(jnp.exp(-jnp.abs(x)))
    lft = lf.T
    carry = jnp.zeros((LANES, 1), F32)
    for c in range(n_blocks):
        blk = lft[:, c * Q_BLOCK:(c + 1) * Q_BLOCK]
        hi = blk.astype(BF16)
        r1 = blk - hi.astype(F32)
        mid = r1.astype(BF16)
        lo = (r1 - mid.astype(F32)).astype(BF16)
        cs = _dot(hi, tri_ref[...]) + _dot(mid, tri_ref[...]) + _dot(lo, tri_ref[...]) + carry
        o_ref[:, c * Q_BLOCK:(c + 1) * Q_BLOCK] = cs[:N_HEADS]
        carry = cs[:, Q_BLOCK - 1:Q_BLOCK]


def _fox_cum(flog, b_pad, batch, seq):
    a = np.arange(Q_BLOCK)
    tri = jnp.asarray((a[:, None] <= a[None, :]).astype(np.float32), BF16)
    return pl.pallas_call(
        functools.partial(_fox_cum_kernel, n_blocks=seq // Q_BLOCK),
        grid=(batch,),
        in_specs=[pl.BlockSpec((None, seq, LANES), lambda b: (b, 0, 0)),
                  _const_spec((1, LANES)), _const_spec((Q_BLOCK, Q_BLOCK))],
        out_specs=pl.BlockSpec((None, N_HEADS, seq), lambda b: (b, 0, 0)),
        out_shape=jax.ShapeDtypeStruct((batch, N_HEADS, seq), F32),
        compiler_params=_params(1),
        name="fox_cum",
    )(flog, b_pad, tri)


def _fox_kernel(q_ref, k_ref, v_ref, c_ref, o_ref, ks_ref, vs_ref, sa_ref, sb_ref, acc_ref, l_ref, m_ref,
                *, n_blocks):
    sel_k = _head_masks(FOX_K)
    k_per_q = FOX_Q // FOX_K

    def stack_heads(j, _):
        rows = pl.ds(pl.multiple_of(j * FOX_K, FOX_K), FOX_K)
        kj, vj = k_ref[rows, :], v_ref[rows, :]
        ks_ref[j] = jnp.concatenate([kj * sel_k[0], kj * sel_k[1]], axis=0)
        vs_ref[j] = jnp.concatenate(
            [jnp.concatenate([vj * sel_k[0], sel_k[0]], axis=1),
             jnp.concatenate([vj * sel_k[1], sel_k[1]], axis=1)], axis=0)
        return 0

    lax.fori_loop(0, n_blocks * k_per_q, stack_heads, 0)

    def scores(q_off, j, r0):
        q = q_ref[pl.ds(q_off + r0, FOX_Q - r0), :] * jnp.asarray(SCALE, BF16)
        return _dot_t(q, ks_ref[j])

    def consume(s2, q_off, j, r0, masked):
        rows = slice(r0, FOX_Q)
        n_rows = FOX_Q - r0
        head0 = lax.broadcasted_iota(jnp.int32, (n_rows, LANES), 1) < HEAD_DIM
        k_off = j * FOX_K
        ps, alphas = [], []
        for h in range(HEADS_PER_TILE):
            cj = c_ref[h, :, pl.ds(k_off, FOX_K)]
            s = s2[:, h * FOX_K:(h + 1) * FOX_K] - cj
            if masked:
                col_minus_row = (lax.broadcasted_iota(jnp.int32, (n_rows, FOX_K), 1)
                                 - lax.broadcasted_iota(jnp.int32, (n_rows, FOX_K), 0))
                s = jnp.where(col_minus_row <= q_off + r0 - k_off, s, NEG_INF)
            m_old = m_ref[h, rows]
            m_blk = jnp.max(jnp.maximum(s[:, :LANES], s[:, LANES:]), axis=-1, keepdims=True)
            m_new = jnp.maximum(m_old, jnp.broadcast_to(m_blk, (n_rows, LANES)))
            m_ref[h, rows] = m_new
            alphas.append(jnp.exp(m_old - m_new))
            ps.append(jnp.concatenate([jnp.exp(s[:, :LANES] - m_new),
                                       jnp.exp(s[:, LANES:] - m_new)], axis=1).astype(BF16))
        alpha = jnp.where(head0, alphas[0], alphas[1])
        pv = _dot(jnp.concatenate(ps, axis=1), vs_ref[j])
        acc_ref[rows] = acc_ref[rows] * alpha + pv[:, :LANES]
        l_ref[rows] = l_ref[rows] * alpha + pv[:, LANES:]

    def q_block(i):
        q_off = i * FOX_Q
        acc_ref[...] = jnp.zeros_like(acc_ref)
        l_ref[...] = jnp.zeros_like(l_ref)
        m_ref[...] = jnp.full_like(m_ref, NEG_INF)
        sa_ref[...] = scores(q_off, 0, 0)

        for p in range(i):
            j = 2 * p
            s2 = sa_ref[...]
            sb_ref[...] = scores(q_off, j + 1, 0)
            consume(s2, q_off, j, 0, False)
            s2 = sb_ref[...]
            sa_ref[...] = scores(q_off, j + 2, 0)
            consume(s2, q_off, j + 1, 0, False)
        j_diag = i * k_per_q
        consume(sa_ref[...], q_off, j_diag, 0, True)
        for d in range(1, k_per_q):
            consume(scores(q_off, j_diag + d, d * FOX_K), q_off, j_diag + d, d * FOX_K, True)
        o_ref[pl.ds(q_off, FOX_Q), :] = (acc_ref[...] / l_ref[...]).astype(BF16)

    for i in range(n_blocks):
        q_block(i)


def _fox_attention(qkv, cum, batch, seq):
    blk = lambda off: pl.BlockSpec((None, seq, LANES), lambda b, t: (b, 0, off + t))
    return pl.pallas_call(
        functools.partial(_fox_kernel, n_blocks=seq // FOX_Q),
        grid=(batch, N_HEAD_TILES),
        in_specs=[blk(0), blk(N_HEAD_TILES), blk(2 * N_HEAD_TILES),
                  pl.BlockSpec((None, HEADS_PER_TILE, 1, seq), lambda b, t: (b, t, 0, 0))],
        out_specs=pl.BlockSpec((None, seq, LANES), lambda b, t: (b, 0, t)),
        out_shape=jax.ShapeDtypeStruct((batch, seq, ATTN_WIDTH), BF16),
        scratch_shapes=[pltpu.VMEM((seq // FOX_K, 2 * FOX_K, LANES), BF16),
                        pltpu.VMEM((seq // FOX_K, 2 * FOX_K, 2 * LANES), BF16),
                        pltpu.VMEM((FOX_Q, 2 * FOX_K), F32),
                        pltpu.VMEM((FOX_Q, 2 * FOX_K), F32),
                        pltpu.VMEM((FOX_Q, LANES), F32),
                        pltpu.VMEM((FOX_Q, LANES), F32),
                        pltpu.VMEM((HEADS_PER_TILE, FOX_Q, LANES), F32)],
        compiler_params=_params(2),
        name="fox_attention",
    )(qkv, qkv, qkv, cum)


def _swa_kernel(sink_ref, q_ref, k_ref, v_ref, o_ref, ks_ref, vs_ref, *, n_blocks):
    g = pl.program_id(1)
    window = 2 * Q_BLOCK
    sel = _head_masks()
    lane = lax.broadcasted_iota(jnp.int32, (Q_BLOCK, LANES), 1)
    mine = (lane >= g * HEAD_DIM) & (lane < (g + 1) * HEAD_DIM)

    def stack_heads(j, _):
        rows = pl.ds(pl.multiple_of(j * Q_BLOCK, Q_BLOCK), Q_BLOCK)
        kg = jnp.where(mine, k_ref[rows, :].astype(F32), 0.0)
        vg = jnp.where(mine, v_ref[rows, :].astype(F32), 0.0)
        k2 = (kg + pltpu.roll(kg, HEAD_DIM, 1)).astype(BF16)
        v2 = (vg + pltpu.roll(vg, HEAD_DIM, 1)).astype(BF16)
        for h in range(HEADS_PER_TILE):
            ks_ref[h, j] = k2 * sel[h]
            vs_ref[h, j] = jnp.concatenate([v2 * sel[h], sel[h]], axis=1)
        return 0

    lax.fori_loop(0, n_blocks, stack_heads, 0)

    a = lax.broadcasted_iota(jnp.int32, (Q_BLOCK, window), 0)
    c = lax.broadcasted_iota(jnp.int32, (Q_BLOCK, window), 1)
    in_window = (c > a) & (c <= a + SWA_WINDOW)
    head0 = lane < HEAD_DIM

    def q_block(n, _):
        rows = pl.ds(pl.multiple_of(n * Q_BLOCK, Q_BLOCK), Q_BLOCK)
        prev = jnp.maximum(n - 1, 0)
        mask = in_window & (c + n * Q_BLOCK >= Q_BLOCK)
        kstack = jnp.concatenate([ks_ref[0, prev], ks_ref[0, n], ks_ref[1, prev], ks_ref[1, n]], axis=0)
        vext = jnp.concatenate([vs_ref[0, prev], vs_ref[0, n], vs_ref[1, prev], vs_ref[1, n]], axis=0)
        for t in range(SWA_GROUP // HEADS_PER_TILE):
            qt = q_ref[rows, t * LANES:(t + 1) * LANES] * jnp.asarray(SCALE, BF16)
            s2 = _dot_t(qt, kstack)
            es, sink_terms = [], []
            for h in range(HEADS_PER_TILE):
                s = jnp.where(mask, s2[:, h * window:(h + 1) * window], NEG_INF)
                sink = sink_ref[g * SWA_GROUP + t * HEADS_PER_TILE + h]
                m = jnp.max(jnp.maximum(s[:, :LANES], s[:, LANES:]), axis=-1, keepdims=True)
                m = jnp.maximum(jnp.broadcast_to(m, (Q_BLOCK, LANES)), sink)
                es.append(jnp.concatenate([jnp.exp(s[:, :LANES] - m),
                                           jnp.exp(s[:, LANES:] - m)], axis=1).astype(BF16))
                sink_terms.append(jnp.exp(sink - m))
            pv = _dot(jnp.concatenate(es, axis=1), vext)
            den = pv[:, LANES:] + jnp.where(head0, sink_terms[0], sink_terms[1])
            o_ref[rows, t * LANES:(t + 1) * LANES] = (pv[:, :LANES] / den).astype(BF16)
        return 0

    lax.fori_loop(0, n_blocks, q_block, 0)


def _swa_attention(qkv, sinks, batch, seq):
    nb = seq // Q_BLOCK
    gw = SWA_GROUP * HEAD_DIM
    k_tile = ATTN_WIDTH // LANES
    kv = lambda tile: pl.BlockSpec((None, seq, LANES), lambda b, g, s: (b, 0, tile))
    grid_spec = pltpu.PrefetchScalarGridSpec(
        num_scalar_prefetch=1,
        grid=(batch, SWA_KV_HEADS),
        in_specs=[pl.BlockSpec((None, seq, gw), lambda b, g, s: (b, 0, g)), kv(k_tile), kv(k_tile + 1)],
        out_specs=pl.BlockSpec((None, seq, gw), lambda b, g, s: (b, 0, g)),
        scratch_shapes=[pltpu.VMEM((HEADS_PER_TILE, nb, Q_BLOCK, LANES), BF16),
                        pltpu.VMEM((HEADS_PER_TILE, nb, Q_BLOCK, 2 * LANES), BF16)],
    )
    return pl.pallas_call(
        functools.partial(_swa_kernel, n_blocks=nb),
        grid_spec=grid_spec,
        out_shape=jax.ShapeDtypeStruct((batch, seq, ATTN_WIDTH), BF16),
        compiler_params=_params(2),
        name="swa_attention",
    )(sinks, qkv, qkv, qkv)


def _tail_kernel(x_ref, a_ref, p_ref, wo_ref, mg_ref, wu_ref, wd_ref, pg_ref, wg_ref, wp_ref,
                 fg_ref, o_ref, h_ref, *, final):
    x1 = x_ref[...] + _dot(a_ref[...], wo_ref[...].astype(BF16))
    h_ref[...] = _rms(x1, mg_ref[...]).astype(BF16)
    x2 = x1
    for c in range(0, D_FF, FF_CHUNK):
        u = jnp.maximum(_dot(h_ref[...], wu_ref[:, c:c + FF_CHUNK].astype(BF16)), 0.0)
        x2 = x2 + _dot((u * u).astype(BF16), wd_ref[c:c + FF_CHUNK, :])
    h3 = _rms(x2, pg_ref[...]).astype(BF16)
    gate = 1.0 / (1.0 + jnp.exp(-_dot(h3, wg_ref[...].astype(BF16))))
    x3 = x2 + _dot(p_ref[...].astype(BF16), wp_ref[...].astype(BF16)) * gate
    o_ref[...] = _rms(x3, fg_ref[...]) if final else x3


def _tail(x2d, a2d, p3d, layer, wo, mg, wu, wd, pg, wg, wp, fg, final):
    t, d = x2d.shape
    tok = lambda w: pl.BlockSpec((TOKEN_TILE, w), lambda i: (i, 0))
    return pl.pallas_call(
        functools.partial(_tail_kernel, final=final),
        grid=(t // TOKEN_TILE,),
        in_specs=[tok(d), tok(ATTN_WIDTH),
                  pl.BlockSpec((None, TOKEN_TILE, PLE_DIM), lambda i: (layer, i, 0)),
                  _const_spec((ATTN_WIDTH, d)), _const_spec((1, d)),
                  _const_spec((d, D_FF)), _const_spec((D_FF, d)), _const_spec((1, d)),
                  _const_spec((d, d)), _const_spec((PLE_DIM, d)), _const_spec((1, d))],
        out_specs=tok(d),
        out_shape=jax.ShapeDtypeStruct((t, d), F32),
        scratch_shapes=[pltpu.VMEM((TOKEN_TILE, d), BF16)],
        compiler_params=_params(1),
        name="layer_tail",
    )(x2d, a2d, p3d, wo, mg, wu, wd, pg, wg, wp, fg)


def kernel(x, p, attn_norm_0, w_in_0, w_out_0, mlp_norm_0, w_up_0, w_down_0, ple_norm_0, w_ple_gate_0, w_ple_proj_0, attn_norm_1, w_in_1, w_out_1, sinks_1, mlp_norm_1, w_up_1, w_down_1, ple_norm_1, w_ple_gate_1, w_ple_proj_1, attn_norm_2, w_in_2, w_out_2, b_forget_2, mlp_norm_2, w_up_2, w_down_2, ple_norm_2, w_ple_gate_2, w_ple_proj_2, attn_norm_3, w_in_3, w_out_3, mlp_norm_3, w_up_3, w_down_3, ple_norm_3, w_ple_gate_3, w_ple_proj_3, final_norm):
    batch, seq, d = x.shape
    t = batch * seq
    layers = [
        (attn_norm_0, w_in_0, w_out_0, None, mlp_norm_0, w_up_0, w_down_0, ple_norm_0, w_ple_gate_0, w_ple_proj_0),
        (attn_norm_1, w_in_1, w_out_1, sinks_1, mlp_norm_1, w_up_1, w_down_1, ple_norm_1, w_ple_gate_1, w_ple_proj_1),
        (attn_norm_2, w_in_2, w_out_2, b_forget_2, mlp_norm_2, w_up_2, w_down_2, ple_norm_2, w_ple_gate_2, w_ple_proj_2),
        (attn_norm_3, w_in_3, w_out_3, None, mlp_norm_3, w_up_3, w_down_3, ple_norm_3, w_ple_gate_3, w_ple_proj_3),
    ]
    row = lambda g: g.reshape(1, d)
    bf = lambda w: w.astype(BF16)
    x2d = x.reshape(t, d)
    p3d = p.reshape(p.shape[0], t, PLE_DIM)
    fg = row(final_norm)
    for i, (an, wi, wo, extra, mn, wu, wd, pn, wg, wp) in enumerate(layers):
        kind = i % 3
        if kind == 0:
            qkv = _inproj(x2d, row(an), wi)
            a = _sb_attention(qkv.reshape(batch, seq, -1), batch, seq)
        elif kind == 1:
            cos, s1, s2 = _rope_tables(seq)
            qkv = _inproj_swa(x2d, row(an), wi, cos, s1, s2, seq)
            a = _swa_attention(qkv.reshape(batch, seq, -1), extra.astype(F32), batch, seq)
        else:
            wf = jnp.pad(wi[:, 3 * ATTN_WIDTH:], ((0, 0), (0, LANES - N_HEADS)))
            qkv, flog = _inproj_fox(x2d, row(an), wi, bf(wf))
            b_pad = jnp.pad(extra.astype(F32), (0, LANES - N_HEADS)).reshape(1, LANES)
            cum = _fox_cum(flog.reshape(batch, seq, LANES), b_pad, batch, seq)
            a = _fox_attention(qkv.reshape(batch, seq, -1),
                               cum.reshape(batch, N_HEADS, 1, seq), batch, seq)
        x2d = _tail(x2d, a.reshape(t, ATTN_WIDTH), p3d, i,
                    wo, row(mn), wu, bf(wd), row(pn), wg, wp, fg,
                    final=(i == len(layers) - 1))
    return x2d.reshape(batch, seq, d)
```

```python
import functools

import jax
import jax.numpy as jnp
import numpy as np
from jax import lax
from jax.experimental import pallas as pl
from jax.experimental.pallas import tpu as pltpu

D_MODEL = 1024
HEAD_DIM = 64
N_HEADS = 16
ATTN_WIDTH = N_HEADS * HEAD_DIM
SWA_KV_HEADS = 2
SWA_GROUP = N_HEADS // SWA_KV_HEADS
SWA_WINDOW = 128
ROPE_THETA = 500000.0
ROPE_DIM = 16
D_FF = 4 * D_MODEL
PLE_DIM = 256
Q_BLOCK = 128
RMS_EPS = 1e-6
NEG_INF = -1e30
SCALE = HEAD_DIM ** -0.5

LANES = 128
HEADS_PER_TILE = LANES // HEAD_DIM
N_HEAD_TILES = ATTN_WIDTH // LANES
TOKEN_TILE = 512
FF_CHUNK = 1024
VMEM_LIMIT = 56 * 1024 * 1024

SB_Q = 2 * Q_BLOCK
SB_EXTRA = 2
SB_EXIT = 104.0
LOG2E = 1.4426950408889634
FOX_Q = 512
FOX_K = 256

assert FOX_K == 2 * LANES and FOX_Q == 2 * FOX_K
assert SB_Q == 2 * Q_BLOCK and SB_EXTRA <= SB_Q // Q_BLOCK

F32 = jnp.float32
BF16 = jnp.bfloat16


def _params(n_axes):
    return pltpu.CompilerParams(dimension_semantics=("parallel",) * n_axes,
                                vmem_limit_bytes=VMEM_LIMIT)


def _const_spec(shape):
    nd = len(shape)
    return pl.BlockSpec(shape, lambda *_: (0,) * nd, pipeline_mode=pl.Buffered(1))


def _rms(x, g):
    ms = jnp.mean(x * x, axis=-1, keepdims=True)
    return (x * lax.rsqrt(ms + RMS_EPS)) * g


def _dot(a, b):
    return jnp.dot(a, b, preferred_element_type=F32)


def _dot_t(a, b):
    return lax.dot_general(a, b, (((1,), (1,)), ((), ())), preferred_element_type=F32)


def _head_masks(rows=Q_BLOCK):
    lane = lax.broadcasted_iota(jnp.int32, (rows, LANES), 1)
    m0 = jnp.where(lane < HEAD_DIM, 1.0, 0.0).astype(BF16)
    m1 = jnp.where(lane < HEAD_DIM, 0.0, 1.0).astype(BF16)
    return m0, m1


def _inproj_kernel(x_ref, g_ref, w_ref, o_ref, h_ref, *, n_out):
    h_ref[...] = _rms(x_ref[...], g_ref[...]).astype(BF16)
    for c in range(0, n_out, FF_CHUNK):
        o_ref[:, c:c + FF_CHUNK] = _dot(h_ref[...], w_ref[:, c:c + FF_CHUNK].astype(BF16)).astype(BF16)


def _inproj(x2d, g, w):
    t, d = x2d.shape
    n_out = w.shape[1]
    return pl.pallas_call(
        functools.partial(_inproj_kernel, n_out=n_out),
        grid=(t // TOKEN_TILE,),
        in_specs=[pl.BlockSpec((TOKEN_TILE, d), lambda i: (i, 0)),
                  _const_spec((1, d)), _const_spec((d, n_out))],
        out_specs=pl.BlockSpec((TOKEN_TILE, n_out), lambda i: (i, 0)),
        out_shape=jax.ShapeDtypeStruct((t, n_out), BF16),
        scratch_shapes=[pltpu.VMEM((TOKEN_TILE, d), BF16)],
        compiler_params=_params(1),
        name="inproj",
    )(x2d, g, w)


def _inproj_fox_kernel(x_ref, g_ref, w_ref, wf_ref, o_ref, f_ref, h_ref, *, n_out):
    h_ref[...] = _rms(x_ref[...], g_ref[...]).astype(BF16)
    for c in range(0, n_out, FF_CHUNK):
        o_ref[:, c:c + FF_CHUNK] = _dot(h_ref[...], w_ref[:, c:c + FF_CHUNK].astype(BF16)).astype(BF16)
    f_ref[...] = _dot(h_ref[...], wf_ref[...])


def _inproj_fox(x2d, g, w, wf):
    t, d = x2d.shape
    n_out = 3 * ATTN_WIDTH
    return pl.pallas_call(
        functools.partial(_inproj_fox_kernel, n_out=n_out),
        grid=(t // TOKEN_TILE,),
        in_specs=[pl.BlockSpec((TOKEN_TILE, d), lambda i: (i, 0)),
                  _const_spec((1, d)), _const_spec(w.shape), _const_spec((d, LANES))],
        out_specs=[pl.BlockSpec((TOKEN_TILE, n_out), lambda i: (i, 0)),
                   pl.BlockSpec((TOKEN_TILE, LANES), lambda i: (i, 0))],
        out_shape=[jax.ShapeDtypeStruct((t, n_out), BF16),
                   jax.ShapeDtypeStruct((t, LANES), F32)],
        scratch_shapes=[pltpu.VMEM((TOKEN_TILE, d), BF16)],
        compiler_params=_params(1),
        name="inproj_fox",
    )(x2d, g, w, wf)


def _inproj_swa_kernel(x_ref, g_ref, w_ref, c_ref, s1_ref, s2_ref, o_ref, h_ref, *, n_rope, n_out):
    h_ref[...] = _rms(x_ref[...], g_ref[...]).astype(BF16)
    cos, s1, s2 = c_ref[...], s1_ref[...], s2_ref[...]
    for c in range(0, n_out, LANES):
        y = _dot(h_ref[...], w_ref[:, c:c + LANES].astype(BF16))
        if c < n_rope:
            y = y * cos + pltpu.roll(y, 8, 1) * s1 + pltpu.roll(y, LANES - 8, 1) * s2
        o_ref[:, c:c + LANES] = y.astype(BF16)


def _inproj_swa(x2d, g, w, cos, s1, s2, seq):
    t, d = x2d.shape
    n_out = w.shape[1]
    n_rope = ATTN_WIDTH + SWA_KV_HEADS * HEAD_DIM
    tiles_per_seq = seq // TOKEN_TILE
    tab = pl.BlockSpec((TOKEN_TILE, LANES), lambda i: (i % tiles_per_seq, 0))
    return pl.pallas_call(
        functools.partial(_inproj_swa_kernel, n_rope=n_rope, n_out=n_out),
        grid=(t // TOKEN_TILE,),
        in_specs=[pl.BlockSpec((TOKEN_TILE, d), lambda i: (i, 0)),
                  _const_spec((1, d)), _const_spec((d, n_out)), tab, tab, tab],
        out_specs=pl.BlockSpec((TOKEN_TILE, n_out), lambda i: (i, 0)),
        out_shape=jax.ShapeDtypeStruct((t, n_out), BF16),
        scratch_shapes=[pltpu.VMEM((TOKEN_TILE, d), BF16)],
        compiler_params=_params(1),
        name="inproj_swa",
    )(x2d, g, w, cos, s1, s2)


def _rope_tables(seq):
    half = ROPE_DIM // 2
    inv_freq = ROPE_THETA ** (-jnp.arange(half, dtype=F32) / half)
    ang = jnp.arange(seq, dtype=jnp.int32).astype(F32)[:, None] * inv_freq[None, :]
    cos, sin = jnp.cos(ang), jnp.sin(ang)
    one = jnp.ones((seq, HEAD_DIM - ROPE_DIM), F32)
    zero = jnp.zeros((seq, HEAD_DIM - ROPE_DIM), F32)
    zh = jnp.zeros((seq, half), F32)
    c_head = jnp.concatenate([cos, cos, one], axis=1)
    s1_head = jnp.concatenate([zh, sin, zero], axis=1)
    s2_head = jnp.concatenate([-sin, zh, zero], axis=1)
    tile = lambda a: jnp.concatenate([a] * HEADS_PER_TILE, axis=1)
    return tile(c_head), tile(s1_head), tile(s2_head)


def _sb_kernel(q_ref, k_ref, v_ref, uu_ref, o_ref, ks_ref, vs_ref, za_ref, zb_ref, acc_ref, r_ref,
               rmin_ref, *, n_blocks):
    sel = _head_masks()
    blocks_per_q = SB_Q // Q_BLOCK
    n_static = blocks_per_q + SB_EXTRA
    scale = jnp.asarray(SCALE, BF16)

    def stack_heads(j, _):
        rows = pl.ds(pl.multiple_of(j * Q_BLOCK, Q_BLOCK), Q_BLOCK)
        kj, vj = k_ref[rows, :], v_ref[rows, :]
        ks_ref[j] = jnp.concatenate([kj * sel[0], kj * sel[1]], axis=0)
        vs_ref[j] = jnp.concatenate([vj * sel[0], vj * sel[1]], axis=0)
        return 0

    lax.fori_loop(0, n_blocks * blocks_per_q, stack_heads, 0)

    def strict(n_rows):
        return (lax.broadcasted_iota(jnp.int32, (n_rows, Q_BLOCK), 1)
                < lax.broadcasted_iota(jnp.int32, (n_rows, Q_BLOCK), 0))

    def scores_ahead(i):
        q = q_ref[pl.ds(i * SB_Q, SB_Q), :] * scale
        ks = ks_ref[pl.ds(i * blocks_per_q - SB_EXTRA, n_static)]
        return _dot_t(q, ks.reshape(n_static * 2 * Q_BLOCK, LANES))

    def key_block(z2, j, r_in, keep):
        ws, r_out = [], []
        for h in range(HEADS_PER_TILE):
            z = z2[:, h * Q_BLOCK:(h + 1) * Q_BLOCK]
            sp = jnp.maximum(z, 0.0) + jnp.log(1.0 + jnp.exp2(jnp.abs(z) * -LOG2E))
            spm = sp if keep is None else jnp.where(keep, sp, 0.0)
            hi = spm.astype(BF16)
            lo = (spm - hi.astype(F32)).astype(BF16)
            cs = _dot(jnp.concatenate([hi, lo], axis=1), uu_ref[...])
            tail, tot = cs[:, :Q_BLOCK], cs[:, Q_BLOCK:]
            if r_in is not None:
                tail, tot = tail + r_in[h], tot + r_in[h]
            w = jnp.exp2((z - tail) * LOG2E)
            if keep is not None:
                w = jnp.where(keep, w, 0.0)
            ws.append(w.astype(BF16))
            r_out.append(tot)
        return _dot(jnp.concatenate(ws, axis=1), vs_ref[j]), r_out

    def first_blocks(i, n_extra, z_ref):
        q_off = i * SB_Q
        j_diag = i * blocks_per_q

        def z_of(d, r0=0):
            if z_ref is None:
                return _dot_t(q_ref[pl.ds(q_off + r0, SB_Q - r0), :] * scale, ks_ref[j_diag - d])
            c0 = (SB_EXTRA - d) * 2 * Q_BLOCK
            return z_ref[r0:, c0:c0 + 2 * Q_BLOCK]

        zeros = jnp.zeros((Q_BLOCK, LANES), F32)
        pv_lo, r_lo = key_block(z_of(-1, Q_BLOCK), j_diag + 1, None, strict(Q_BLOCK))
        r = [jnp.concatenate([zeros, r_lo[h]], axis=0) for h in range(HEADS_PER_TILE)]
        pv, r = key_block(z_of(0), j_diag, r, strict(SB_Q))
        acc = pv + jnp.concatenate([zeros, pv_lo], axis=0)
        for d in range(1, n_extra + 1):
            pv, r = key_block(z_of(d), j_diag - d, r, None)
            acc = acc + pv
        acc_ref[pl.ds(q_off, SB_Q), :] = acc
        for h in range(HEADS_PER_TILE):
            r_ref[h, pl.ds(q_off, SB_Q), :] = r[h]
        rmin_ref[i] = jnp.min(jnp.minimum(r[0], r[1]))

    def remaining_blocks(i, _):
        rows = pl.ds(pl.multiple_of(i * SB_Q, SB_Q), SB_Q)
        q = q_ref[rows, :] * scale

        def more(c):
            j, r_min = c
            return (j >= 0) & (r_min <= SB_EXIT)

        def step(c):
            j, _ = c
            pv, r_new = key_block(_dot_t(q, ks_ref[j]), j,
                                  [r_ref[h, rows, :] for h in range(HEADS_PER_TILE)], None)
            acc_ref[rows, :] += pv
            for h in range(HEADS_PER_TILE):
                r_ref[h, rows, :] = r_new[h]
            return j - 1, jnp.min(jnp.minimum(r_new[0], r_new[1]))

        lax.while_loop(more, step, (i * blocks_per_q - SB_EXTRA - 1, rmin_ref[i]))
        o_ref[rows, :] = acc_ref[rows, :].astype(BF16)
        return 0

    z_refs = (za_ref, zb_ref)
    for i in range(n_blocks):
        if i + 1 < n_blocks:
            z_refs[(i + 1) % 2][...] = scores_ahead(i + 1)
        first_blocks(i, SB_EXTRA if i else 0, z_refs[i % 2] if i else None)
    o_ref[pl.ds(0, SB_Q), :] = acc_ref[pl.ds(0, SB_Q), :].astype(BF16)
    lax.fori_loop(1, n_blocks, remaining_blocks, 0)


def _cumsum_weights():
    a = np.arange(Q_BLOCK)
    u = (a[:, None] >= a[None, :]).astype(np.float32)
    half = np.concatenate([u, np.ones_like(u)], axis=1)
    return jnp.asarray(np.concatenate([half, half], axis=0), BF16)


def _sb_attention(qkv, batch, seq):
    blk = lambda off: pl.BlockSpec((None, seq, LANES), lambda b, t: (b, 0, off + t))
    return pl.pallas_call(
        functools.partial(_sb_kernel, n_blocks=seq // SB_Q),
        grid=(batch, N_HEAD_TILES),
        in_specs=[blk(0), blk(N_HEAD_TILES), blk(2 * N_HEAD_TILES),
                  _const_spec((2 * Q_BLOCK, 2 * Q_BLOCK))],
        out_specs=pl.BlockSpec((None, seq, LANES), lambda b, t: (b, 0, t)),
        out_shape=jax.ShapeDtypeStruct((batch, seq, ATTN_WIDTH), BF16),
        scratch_shapes=[pltpu.VMEM((seq // Q_BLOCK, 2 * Q_BLOCK, LANES), BF16),
                        pltpu.VMEM((seq // Q_BLOCK, 2 * Q_BLOCK, LANES), BF16),
                        pltpu.VMEM((SB_Q, (SB_Q // Q_BLOCK + SB_EXTRA) * 2 * Q_BLOCK), F32),
                        pltpu.VMEM((SB_Q, (SB_Q // Q_BLOCK + SB_EXTRA) * 2 * Q_BLOCK), F32),
                        pltpu.VMEM((seq, LANES), F32),
                        pltpu.VMEM((HEADS_PER_TILE, seq, Q_BLOCK), F32),
                        pltpu.SMEM((seq // SB_Q,), F32)],
        compiler_params=_params(2),
        name="sb_attention",
    )(qkv, qkv, qkv, _cumsum_weights())


def _fox_cum_kernel(f_ref, b_ref, tri_ref, o_ref, *, n_blocks):
    x = f_ref[...] + b_ref[...]
    lf = jnp.minimum(x, 0.0) - jnp.log1p(jnp.exp(-jnp.abs(x)))
    lft = lf.T
    carry = jnp.zeros((LANES, 1), F32)
    for c in range(n_blocks):
        blk = lft[:, c * Q_BLOCK:(c + 1) * Q_BLOCK]
        hi = blk.astype(BF16)
        r1 = blk - hi.astype(F32)
        mid = r1.astype(BF16)
        lo = (r1 - mid.astype(F32)).astype(BF16)
        cs = _dot(hi, tri_ref[...]) + _dot(mid, tri_ref[...]) + _dot(lo, tri_ref[...]) + carry
        o_ref[:, c * Q_BLOCK:(c + 1) * Q_BLOCK] = cs[:N_HEADS]
        carry = cs[:, Q_BLOCK - 1:Q_BLOCK]


def _fox_cum(flog, b_pad, batch, seq):
    a = np.arange(Q_BLOCK)
    tri = jnp.asarray((a[:, None] <= a[None, :]).astype(np.float32), BF16)
    return pl.pallas_call(
        functools.partial(_fox_cum_kernel, n_blocks=seq // Q_BLOCK),
        grid=(batch,),
        in_specs=[pl.BlockSpec((None, seq, LANES), lambda b: (b, 0, 0)),
                  _const_spec((1, LANES)), _const_spec((Q_BLOCK, Q_BLOCK))],
        out_specs=pl.BlockSpec((None, N_HEADS, seq), lambda b: (b, 0, 0)),
        out_shape=jax.ShapeDtypeStruct((batch, N_HEADS, seq), F32),
        compiler_params=_params(1),
        name="fox_cum",
    )(flog, b_pad, tri)


def _fox_kernel(q_ref, k_ref, v_ref, c_ref, o_ref, ks_ref, vs_ref, sa_ref, sb_ref, acc_ref, l_ref, m_ref,
                *, n_blocks):
    sel_k = _head_masks(FOX_K)
    k_per_q = FOX_Q // FOX_K

    def stack_heads(j, _):
        rows = pl.ds(pl.multiple_of(j * FOX_K, FOX_K), FOX_K)
        kj, vj = k_ref[rows, :], v_ref[rows, :]
        ks_ref[j] = jnp.concatenate([kj * sel_k[0], kj * sel_k[1]], axis=0)
        vs_ref[j] = jnp.concatenate(
            [jnp.concatenate([vj * sel_k[0], sel_k[0]], axis=1),
             jnp.concatenate([vj * sel_k[1], sel_k[1]], axis=1)], axis=0)
        return 0

    lax.fori_loop(0, n_blocks * k_per_q, stack_heads, 0)

    def scores(q_off, j, r0):
        q = q_ref[pl.ds(q_off + r0, FOX_Q - r0), :] * jnp.asarray(SCALE, BF16)
        return _dot_t(q, ks_ref[j])

    def consume(s2, q_off, j, r0, masked):
        rows = slice(r0, FOX_Q)
        n_rows = FOX_Q - r0
        head0 = lax.broadcasted_iota(jnp.int32, (n_rows, LANES), 1) < HEAD_DIM
        k_off = j * FOX_K
        ps, alphas = [], []
        for h in range(HEADS_PER_TILE):
            cj = c_ref[h, :, pl.ds(k_off, FOX_K)]
            s = s2[:, h * FOX_K:(h + 1) * FOX_K] - cj
            if masked:
                col_minus_row = (lax.broadcasted_iota(jnp.int32, (n_rows, FOX_K), 1)
                                 - lax.broadcasted_iota(jnp.int32, (n_rows, FOX_K), 0))
                s = jnp.where(col_minus_row <= q_off + r0 - k_off, s, NEG_INF)
            m_old = m_ref[h, rows]
            m_blk = jnp.max(jnp.maximum(s[:, :LANES], s[:, LANES:]), axis=-1, keepdims=True)
            m_new = jnp.maximum(m_old, jnp.broadcast_to(m_blk, (n_rows, LANES)))
            m_ref[h, rows] = m_new
            alphas.append(jnp.exp(m_old - m_new))
            ps.append(jnp.concatenate([jnp.exp(s[:, :LANES] - m_new),
                                       jnp.exp(s[:, LANES:] - m_new)], axis=1).astype(BF16))
        alpha = jnp.where(head0, alphas[0], alphas[1])
        pv = _dot(jnp.concatenate(ps, axis=1), vs_ref[j])
        acc_ref[rows] = acc_ref[rows] * alpha + pv[:, :LANES]
        l_ref[rows] = l_ref[rows] * alpha + pv[:, LANES:]

    def q_block(i):
        q_off = i * FOX_Q
        acc_ref[...] = jnp.zeros_like(acc_ref)
        l_ref[...] = jnp.zeros_like(l_ref)
        m_ref[...] = jnp.full_like(m_ref, NEG_INF)
        sa_ref[...] = scores(q_off, 0, 0)

        for p in range(i):
            j = 2 * p
            s2 = sa_ref[...]
            sb_ref[...] = scores(q_off, j + 1, 0)
            consume(s2, q_off, j, 0, False)
            s2 = sb_ref[...]
            sa_ref[...] = scores(q_off, j + 2, 0)
            consume(s2, q_off, j + 1, 0, False)
        j_diag = i * k_per_q
        consume(sa_ref[...], q_off, j_diag, 0, True)
        for d in range(1, k_per_q):
            consume(scores(q_off, j_diag + d, d * FOX_K), q_off, j_diag + d, d * FOX_K, True)
        o_ref[pl.ds(q_off, FOX_Q), :] = (acc_ref[...] / l_ref[...]).astype(BF16)

    for i in range(n_blocks):
        q_block(i)


def _fox_attention(qkv, cum, batch, seq):
    blk = lambda off: pl.BlockSpec((None, seq, LANES), lambda b, t: (b, 0, off + t))
    return pl.pallas_call(
        functools.partial(_fox_kernel, n_blocks=seq // FOX_Q),
        grid=(batch, N_HEAD_TILES),
        in_specs=[blk(0), blk(N_HEAD_TILES), blk(2 * N_HEAD_TILES),
                  pl.BlockSpec((None, HEADS_PER_TILE, 1, seq), lambda b, t: (b, t, 0, 0))],
        out_specs=pl.BlockSpec((None, seq, LANES), lambda b, t: (b, 0, t)),
        out_shape=jax.ShapeDtypeStruct((batch, seq, ATTN_WIDTH), BF16),
        scratch_shapes=[pltpu.VMEM((seq // FOX_K, 2 * FOX_K, LANES), BF16),
                        pltpu.VMEM((seq // FOX_K, 2 * FOX_K, 2 * LANES), BF16),
                        pltpu.VMEM((FOX_Q, 2 * FOX_K), F32),
                        pltpu.VMEM((FOX_Q, 2 * FOX_K), F32),
                        pltpu.VMEM((FOX_Q, LANES), F32),
                        pltpu.VMEM((FOX_Q, LANES), F32),
                        pltpu.VMEM((HEADS_PER_TILE, FOX_Q, LANES), F32)],
        compiler_params=_params(2),
        name="fox_attention",
    )(qkv, qkv, qkv, cum)


def _swa_kernel(sink_ref, q_ref, k_ref, v_ref, o_ref, ks_ref, vs_ref, *, n_blocks):
    g = pl.program_id(1)
    window = 2 * Q_BLOCK
    sel = _head_masks()
    lane = lax.broadcasted_iota(jnp.int32, (Q_BLOCK, LANES), 1)
    mine = (lane >= g * HEAD_DIM) & (lane < (g + 1) * HEAD_DIM)

    def stack_heads(j, _):
        rows = pl.ds(pl.multiple_of(j * Q_BLOCK, Q_BLOCK), Q_BLOCK)
        kg = jnp.where(mine, k_ref[rows, :].astype(F32), 0.0)
        vg = jnp.where(mine, v_ref[rows, :].astype(F32), 0.0)
        k2 = (kg + pltpu.roll(kg, HEAD_DIM, 1)).astype(BF16)
        v2 = (vg + pltpu.roll(vg, HEAD_DIM, 1)).astype(BF16)
        for h in range(HEADS_PER_TILE):
            ks_ref[h, j] = k2 * sel[h]
            vs_ref[h, j] = jnp.concatenate([v2 * sel[h], sel[h]], axis=1)
        return 0

    lax.fori_loop(0, n_blocks, stack_heads, 0)

    a = lax.broadcasted_iota(jnp.int32, (Q_BLOCK, window), 0)
    c = lax.broadcasted_iota(jnp.int32, (Q_BLOCK, window), 1)
    in_window = (c > a) & (c <= a + SWA_WINDOW)
    head0 = lane < HEAD_DIM

    def q_block(n):
        rows = pl.ds(n * Q_BLOCK, Q_BLOCK)
        prev = max(n - 1, 0)
        mask = in_window & (c + n * Q_BLOCK >= Q_BLOCK)
        kstack = jnp.concatenate([ks_ref[0, prev], ks_ref[0, n], ks_ref[1, prev], ks_ref[1, n]], axis=0)
        vext = jnp.concatenate([vs_ref[0, prev], vs_ref[0, n], vs_ref[1, prev], vs_ref[1, n]], axis=0)
        for t in range(SWA_GROUP // HEADS_PER_TILE):
            qt = q_ref[rows, t * LANES:(t + 1) * LANES] * jnp.asarray(SCALE, BF16)
            s2 = _dot_t(qt, kstack)
            es, sink_terms = [], []
            for h in range(HEADS_PER_TILE):
                s = jnp.where(mask, s2[:, h * window:(h + 1) * window], NEG_INF)
                sink = sink_ref[g * SWA_GROUP + t * HEADS_PER_TILE + h]
                m = jnp.max(jnp.maximum(s[:, :LANES], s[:, LANES:]), axis=-1, keepdims=True)
                m = jnp.maximum(jnp.broadcast_to(m, (Q_BLOCK, LANES)), sink)
                es.append(jnp.concatenate([jnp.exp(s[:, :LANES] - m),
                                           jnp.exp(s[:, LANES:] - m)], axis=1).astype(BF16))
                sink_terms.append(jnp.exp(sink - m))
            pv = _dot(jnp.concatenate(es, axis=1), vext)
            den = pv[:, LANES:] + jnp.where(head0, sink_terms[0], sink_terms[1])
            o_ref[rows, t * LANES:(t + 1) * LANES] = (pv[:, :LANES] / den).astype(BF16)

    for n in range(n_blocks):
        q_block(n)


def _swa_attention(qkv, sinks, batch, seq):
    nb = seq // Q_BLOCK
    gw = SWA_GROUP * HEAD_DIM
    k_tile = ATTN_WIDTH // LANES
    kv = lambda tile: pl.BlockSpec((None, seq, LANES), lambda b, g, s: (b, 0, tile))
    grid_spec = pltpu.PrefetchScalarGridSpec(
        num_scalar_prefetch=1,
        grid=(batch, SWA_KV_HEADS),
        in_specs=[pl.BlockSpec((None, seq, gw), lambda b, g, s: (b, 0, g)), kv(k_tile), kv(k_tile + 1)],
        out_specs=pl.BlockSpec((None, seq, gw), lambda b, g, s: (b, 0, g)),
        scratch_shapes=[pltpu.VMEM((HEADS_PER_TILE, nb, Q_BLOCK, LANES), BF16),
                        pltpu.VMEM((HEADS_PER_TILE, nb, Q_BLOCK, 2 * LANES), BF16)],
    )
    return pl.pallas_call(
        functools.partial(_swa_kernel, n_blocks=nb),
        grid_spec=grid_spec,
        out_shape=jax.ShapeDtypeStruct((batch, seq, ATTN_WIDTH), BF16),
        compiler_params=_params(2),
        name="swa_attention",
    )(sinks, qkv, qkv, qkv)


def _tail_kernel(x_ref, a_ref, p_ref, wo_ref, mg_ref, wu_ref, wd_ref, pg_ref, wg_ref, wp_ref,
                 fg_ref, o_ref, h_ref, *, final):
    x1 = x_ref[...] + _dot(a_ref[...], wo_ref[...].astype(BF16))
    h_ref[...] = _rms(x1, mg_ref[...]).astype(BF16)
    x2 = x1
    for c in range(0, D_FF, FF_CHUNK):
        u = jnp.maximum(_dot(h_ref[...], wu_ref[:, c:c + FF_CHUNK].astype(BF16)), 0.0)
        x2 = x2 + _dot((u * u).astype(BF16), wd_ref[c:c + FF_CHUNK, :])
    h3 = _rms(x2, pg_ref[...]).astype(BF16)
    gate = 1.0 / (1.0 + jnp.exp(-_dot(h3, wg_ref[...].astype(BF16))))
    x3 = x2 + _dot(p_ref[...].astype(BF16), wp_ref[...].astype(BF16)) * gate
    o_ref[...] = _rms(x3, fg_ref[...]) if final else x3


def _tail(x2d, a2d, p3d, layer, wo, mg, wu, wd, pg, wg, wp, fg, final):
    t, d = x2d.shape
    tok = lambda w: pl.BlockSpec((TOKEN_TILE, w), lambda i: (i, 0))
    return pl.pallas_call(
        functools.partial(_tail_kernel, final=final),
        grid=(t // TOKEN_TILE,),
        in_specs=[tok(d), tok(ATTN_WIDTH),
                  pl.BlockSpec((None, TOKEN_TILE, PLE_DIM), lambda i: (layer, i, 0)),
                  _const_spec((ATTN_WIDTH, d)), _const_spec((1, d)),
                  _const_spec((d, D_FF)), _const_spec((D_FF, d)), _const_spec((1, d)),
                  _const_spec((d, d)), _const_spec((PLE_DIM, d)), _const_spec((1, d))],
        out_specs=tok(d),
        out_shape=jax.ShapeDtypeStruct((t, d), F32),
        scratch_shapes=[pltpu.VMEM((TOKEN_TILE, d), BF16)],
        compiler_params=_params(1),
        name="layer_tail",
    )(x2d, a2d, p3d, wo, mg, wu, wd, pg, wg, wp, fg)


def kernel(x, p, attn_norm_0, w_in_0, w_out_0, mlp_norm_0, w_up_0, w_down_0, ple_norm_0, w_ple_gate_0, w_ple_proj_0, attn_norm_1, w_in_1, w_out_1, sinks_1, mlp_norm_1, w_up_1, w_down_1, ple_norm_1, w_ple_gate_1, w_ple_proj_1, attn_norm_2, w_in_2, w_out_2, b_forget_2, mlp_norm_2, w_up_2, w_down_2, ple_norm_2, w_ple_gate_2, w_ple_proj_2, attn_norm_3, w_in_3, w_out_3, mlp_norm_3, w_up_3, w_down_3, ple_norm_3, w_ple_gate_3, w_ple_proj_3, final_norm):
    batch, seq, d = x.shape
    t = batch * seq
    layers = [
        (attn_norm_0, w_in_0, w_out_0, None, mlp_norm_0, w_up_0, w_down_0, ple_norm_0, w_ple_gate_0, w_ple_proj_0),
        (attn_norm_1, w_in_1, w_out_1, sinks_1, mlp_norm_1, w_up_1, w_down_1, ple_norm_1, w_ple_gate_1, w_ple_proj_1),
        (attn_norm_2, w_in_2, w_out_2, b_forget_2, mlp_norm_2, w_up_2, w_down_2, ple_norm_2, w_ple_gate_2, w_ple_proj_2),
        (attn_norm_3, w_in_3, w_out_3, None, mlp_norm_3, w_up_3, w_down_3, ple_norm_3, w_ple_gate_3, w_ple_proj_3),
    ]
    row = lambda g: g.reshape(1, d)
    bf = lambda w: w.astype(BF16)
    x2d = x.reshape(t, d)
    p3d = p.reshape(p.shape[0], t, PLE_DIM)
    fg = row(final_norm)
    for i, (an, wi, wo, extra, mn, wu, wd, pn, wg, wp) in enumerate(layers):
        kind = i % 3
        if kind == 0:
            qkv = _inproj(x2d, row(an), wi)
            a = _sb_attention(qkv.reshape(batch, seq, -1), batch, seq)
        elif kind == 1:
            cos, s1, s2 = _rope_tables(seq)
            qkv = _inproj_swa(x2d, row(an), wi, cos, s1, s2, seq)
            a = _swa_attention(qkv.reshape(batch, seq, -1), extra.astype(F32), batch, seq)
        else:
            wf = jnp.pad(wi[:, 3 * ATTN_WIDTH:], ((0, 0), (0, LANES - N_HEADS)))
            qkv, flog = _inproj_fox(x2d, row(an), wi, bf(wf))
            b_pad = jnp.pad(extra.astype(F32), (0, LANES - N_HEADS)).reshape(1, LANES)
            cum = _fox_cum(flog.reshape(batch, seq, LANES), b_pad, batch, seq)
            a = _fox_attention(qkv.reshape(batch, seq, -1),
                               cum.reshape(batch, N_HEADS, 1, seq), batch, seq)
        x2d = _tail(x2d, a.reshape(t, ATTN_WIDTH), p3d, i,
                    wo, row(mn), wu, bf(wd), row(pn), wg, wp, fg,
                    final=(i == len(layers) - 1))
    return x2d.reshape(batch, seq, d)
```

```python
import functools

import jax
import jax.numpy as jnp
import numpy as np
from jax import lax
from jax.experimental import pallas as pl
from jax.experimental.pallas import tpu as pltpu

D_MODEL = 1024
HEAD_DIM = 64
N_HEADS = 16
ATTN_WIDTH = N_HEADS * HEAD_DIM
SWA_KV_HEADS = 2
SWA_GROUP = N_HEADS // SWA_KV_HEADS
SWA_WINDOW = 128
ROPE_THETA = 500000.0
ROPE_DIM = 16
D_FF = 4 * D_MODEL
PLE_DIM = 256
Q_BLOCK = 128
RMS_EPS = 1e-6
NEG_INF = -1e30
SCALE = HEAD_DIM ** -0.5

LANES = 128
HEADS_PER_TILE = LANES // HEAD_DIM
N_HEAD_TILES = ATTN_WIDTH // LANES
TOKEN_TILE = 512
FF_CHUNK = 1024
VMEM_LIMIT = 56 * 1024 * 1024

SB_Q = 2 * Q_BLOCK
SB_EXTRA = 2
SB_EXIT = 104.0
LOG2E = 1.4426950408889634
FOX_Q = 512
FOX_K = 256

assert FOX_K == 2 * LANES and FOX_Q == 2 * FOX_K
assert SB_Q == 2 * Q_BLOCK and SB_EXTRA <= SB_Q // Q_BLOCK

F32 = jnp.float32
BF16 = jnp.bfloat16


def _params(n_axes):
    return pltpu.CompilerParams(dimension_semantics=("parallel",) * n_axes,
                                vmem_limit_bytes=VMEM_LIMIT)


def _const_spec(shape):
    nd = len(shape)
    return pl.BlockSpec(shape, lambda *_: (0,) * nd, pipeline_mode=pl.Buffered(1))


def _rms(x, g):
    ms = jnp.mean(x * x, axis=-1, keepdims=True)
    return (x * lax.rsqrt(ms + RMS_EPS)) * g


def _dot(a, b):
    return jnp.dot(a, b, preferred_element_type=F32)


def _dot_t(a, b):
    return lax.dot_general(a, b, (((1,), (1,)), ((), ())), preferred_element_type=F32)


def _head_masks(rows=Q_BLOCK):
    lane = lax.broadcasted_iota(jnp.int32, (rows, LANES), 1)
    m0 = jnp.where(lane < HEAD_DIM, 1.0, 0.0).astype(BF16)
    m1 = jnp.where(lane < HEAD_DIM, 0.0, 1.0).astype(BF16)
    return m0, m1


def _inproj_kernel(x_ref, g_ref, w_ref, o_ref, h_ref, *, n_out):
    h_ref[...] = _rms(x_ref[...], g_ref[...]).astype(BF16)
    for c in range(0, n_out, FF_CHUNK):
        o_ref[:, c:c + FF_CHUNK] = _dot(h_ref[...], w_ref[:, c:c + FF_CHUNK].astype(BF16)).astype(BF16)


def _inproj(x2d, g, w):
    t, d = x2d.shape
    n_out = w.shape[1]
    return pl.pallas_call(
        functools.partial(_inproj_kernel, n_out=n_out),
        grid=(t // TOKEN_TILE,),
        in_specs=[pl.BlockSpec((TOKEN_TILE, d), lambda i: (i, 0)),
                  _const_spec((1, d)), _const_spec((d, n_out))],
        out_specs=pl.BlockSpec((TOKEN_TILE, n_out), lambda i: (i, 0)),
        out_shape=jax.ShapeDtypeStruct((t, n_out), BF16),
        scratch_shapes=[pltpu.VMEM((TOKEN_TILE, d), BF16)],
        compiler_params=_params(1),
        name="inproj",
    )(x2d, g, w)


def _inproj_fox_kernel(x_ref, g_ref, w_ref, wf_ref, o_ref, f_ref, h_ref, *, n_out):
    h_ref[...] = _rms(x_ref[...], g_ref[...]).astype(BF16)
    for c in range(0, n_out, FF_CHUNK):
        o_ref[:, c:c + FF_CHUNK] = _dot(h_ref[...], w_ref[:, c:c + FF_CHUNK].astype(BF16)).astype(BF16)
    f_ref[...] = _dot(h_ref[...], wf_ref[...])


def _inproj_fox(x2d, g, w, wf):
    t, d = x2d.shape
    n_out = 3 * ATTN_WIDTH
    return pl.pallas_call(
        functools.partial(_inproj_fox_kernel, n_out=n_out),
        grid=(t // TOKEN_TILE,),
        in_specs=[pl.BlockSpec((TOKEN_TILE, d), lambda i: (i, 0)),
                  _const_spec((1, d)), _const_spec(w.shape), _const_spec((d, LANES))],
        out_specs=[pl.BlockSpec((TOKEN_TILE, n_out), lambda i: (i, 0)),
                   pl.BlockSpec((TOKEN_TILE, LANES), lambda i: (i, 0))],
        out_shape=[jax.ShapeDtypeStruct((t, n_out), BF16),
                   jax.ShapeDtypeStruct((t, LANES), F32)],
        scratch_shapes=[pltpu.VMEM((TOKEN_TILE, d), BF16)],
        compiler_params=_params(1),
        name="inproj_fox",
    )(x2d, g, w, wf)


def _inproj_swa_kernel(x_ref, g_ref, w_ref, c_ref, s1_ref, s2_ref, o_ref, h_ref, *, n_rope, n_out):
    h_ref[...] = _rms(x_ref[...], g_ref[...]).astype(BF16)
    cos, s1, s2 = c_ref[...], s1_ref[...], s2_ref[...]
    for c in range(0, n_out, LANES):
        y = _dot(h_ref[...], w_ref[:, c:c + LANES].astype(BF16))
        if c < n_rope:
            y = y * cos + pltpu.roll(y, 8, 1) * s1 + pltpu.roll(y, LANES - 8, 1) * s2
        o_ref[:, c:c + LANES] = y.astype(BF16)


def _inproj_swa(x2d, g, w, cos, s1, s2, seq):
    t, d = x2d.shape
    n_out = w.shape[1]
    n_rope = ATTN_WIDTH + SWA_KV_HEADS * HEAD_DIM
    tiles_per_seq = seq // TOKEN_TILE
    tab = pl.BlockSpec((TOKEN_TILE, LANES), lambda i: (i % tiles_per_seq, 0))
    return pl.pallas_call(
        functools.partial(_inproj_swa_kernel, n_rope=n_rope, n_out=n_out),
        grid=(t // TOKEN_TILE,),
        in_specs=[pl.BlockSpec((TOKEN_TILE, d), lambda i: (i, 0)),
                  _const_spec((1, d)), _const_spec((d, n_out)), tab, tab, tab],
        out_specs=pl.BlockSpec((TOKEN_TILE, n_out), lambda i: (i, 0)),
        out_shape=jax.ShapeDtypeStruct((t, n_out), BF16),
        scratch_shapes=[pltpu.VMEM((TOKEN_TILE, d), BF16)],
        compiler_params=_params(1),
        name="inproj_swa",
    )(x2d, g, w, cos, s1, s2)


def _rope_tables(seq):
    half = ROPE_DIM // 2
    inv_freq = ROPE_THETA ** (-jnp.arange(half, dtype=F32) / half)
    ang = jnp.arange(seq, dtype=jnp.int32).astype(F32)[:, None] * inv_freq[None, :]
    cos, sin = jnp.cos(ang), jnp.sin(ang)
    one = jnp.ones((seq, HEAD_DIM - ROPE_DIM), F32)
    zero = jnp.zeros((seq, HEAD_DIM - ROPE_DIM), F32)
    zh = jnp.zeros((seq, half), F32)
    c_head = jnp.concatenate([cos, cos, one], axis=1)
    s1_head = jnp.concatenate([zh, sin, zero], axis=1)
    s2_head = jnp.concatenate([-sin, zh, zero], axis=1)
    tile = lambda a: jnp.concatenate([a] * HEADS_PER_TILE, axis=1)
    return tile(c_head), tile(s1_head), tile(s2_head)


def _sb_kernel(q_ref, k_ref, v_ref, uu_ref, o_ref, ks_ref, vs_ref, za_ref, zb_ref, acc_ref, r_ref,
               rmin_ref, *, n_blocks):
    sel = _head_masks()
    blocks_per_q = SB_Q // Q_BLOCK
    n_static = blocks_per_q + SB_EXTRA
    scale = jnp.asarray(SCALE, BF16)

    def stack_heads(j, _):
        rows = pl.ds(pl.multiple_of(j * Q_BLOCK, Q_BLOCK), Q_BLOCK)
        kj, vj = k_ref[rows, :], v_ref[rows, :]
        ks_ref[j] = jnp.concatenate([kj * sel[0], kj * sel[1]], axis=0)
        vs_ref[j] = jnp.concatenate([vj * sel[0], vj * sel[1]], axis=0)
        return 0

    lax.fori_loop(0, n_blocks * blocks_per_q, stack_heads, 0)

    def strict(n_rows):
        return (lax.broadcasted_iota(jnp.int32, (n_rows, Q_BLOCK), 1)
                < lax.broadcasted_iota(jnp.int32, (n_rows, Q_BLOCK), 0))

    def scores_ahead(i):
        q = q_ref[pl.ds(i * SB_Q, SB_Q), :] * scale
        ks = ks_ref[pl.ds(i * blocks_per_q - SB_EXTRA, n_static)]
        return _dot_t(q, ks.reshape(n_static * 2 * Q_BLOCK, LANES))

    def key_block(z2, r_in, keep):
        ws, r_out = [], []
        for h in range(HEADS_PER_TILE):
            z = z2[:, h * Q_BLOCK:(h + 1) * Q_BLOCK]
            sp = jnp.maximum(z, 0.0) + jnp.log(1.0 + jnp.exp2(jnp.abs(z) * -LOG2E))
            spm = sp if keep is None else jnp.where(keep, sp, 0.0)
            hi = spm.astype(BF16)
            lo = (spm - hi.astype(F32)).astype(BF16)
            cs = _dot(jnp.concatenate([hi, lo], axis=1), uu_ref[...])
            tail, tot = cs[:, :Q_BLOCK], cs[:, Q_BLOCK:]
            if r_in is not None:
                tail, tot = tail + r_in[h], tot + r_in[h]
            w = jnp.exp2((z - tail) * LOG2E)
            if keep is not None:
                w = jnp.where(keep, w, 0.0)
            ws.append(w.astype(BF16))
            r_out.append(tot)
        return jnp.concatenate(ws, axis=1), r_out

    def first_blocks(i, n_extra, z_ref):
        q_off = i * SB_Q
        j_diag = i * blocks_per_q

        def z_of(d, r0=0):
            if z_ref is None:
                return _dot_t(q_ref[pl.ds(q_off + r0, SB_Q - r0), :] * scale, ks_ref[j_diag - d])
            c0 = (SB_EXTRA - d) * 2 * Q_BLOCK
            return z_ref[r0:, c0:c0 + 2 * Q_BLOCK]

        zeros = jnp.zeros((Q_BLOCK, LANES), F32)
        w_lo, r_lo = key_block(z_of(-1, Q_BLOCK), None, strict(Q_BLOCK))
        r = [jnp.concatenate([zeros, r_lo[h]], axis=0) for h in range(HEADS_PER_TILE)]
        w_diag, r = key_block(z_of(0), r, strict(SB_Q))
        ws = [w_diag, jnp.concatenate([jnp.zeros_like(w_lo), w_lo], axis=0)]
        for d in range(1, n_extra + 1):
            w_d, r = key_block(z_of(d), r, None)
            ws.insert(0, w_d)
        vs = vs_ref[pl.ds(j_diag - n_extra, n_extra + blocks_per_q)]
        acc = _dot(jnp.concatenate(ws, axis=1), vs.reshape((n_extra + blocks_per_q) * 2 * Q_BLOCK, LANES))
        acc_ref[pl.ds(q_off, SB_Q), :] = acc
        for h in range(HEADS_PER_TILE):
            r_ref[h, pl.ds(q_off, SB_Q), :] = r[h]
        rmin_ref[i] = jnp.min(jnp.minimum(r[0], r[1]))

    def remaining_blocks(i, _):
        rows = pl.ds(pl.multiple_of(i * SB_Q, SB_Q), SB_Q)
        q = q_ref[rows, :] * scale

        def more(c):
            j, r_min = c
            return (j >= 0) & (r_min <= SB_EXIT)

        def step(c):
            j, _ = c
            w, r_new = key_block(_dot_t(q, ks_ref[j]),
                                 [r_ref[h, rows, :] for h in range(HEADS_PER_TILE)], None)
            acc_ref[rows, :] += _dot(w, vs_ref[j])
            for h in range(HEADS_PER_TILE):
                r_ref[h, rows, :] = r_new[h]
            return j - 1, jnp.min(jnp.minimum(r_new[0], r_new[1]))

        lax.while_loop(more, step, (i * blocks_per_q - SB_EXTRA - 1, rmin_ref[i]))
        o_ref[rows, :] = acc_ref[rows, :].astype(BF16)
        return 0

    z_refs = (za_ref, zb_ref)
    for i in range(n_blocks):
        if i + 1 < n_blocks:
            z_refs[(i + 1) % 2][...] = scores_ahead(i + 1)
        first_blocks(i, SB_EXTRA if i else 0, z_refs[i % 2] if i else None)
    o_ref[pl.ds(0, SB_Q), :] = acc_ref[pl.ds(0, SB_Q), :].astype(BF16)
    lax.fori_loop(1, n_blocks, remaining_blocks, 0)


def _cumsum_weights():
    a = np.arange(Q_BLOCK)
    u = (a[:, None] >= a[None, :]).astype(np.float32)
    half = np.concatenate([u, np.ones_like(u)], axis=1)
    return jnp.asarray(np.concatenate([half, half], axis=0), BF16)


def _sb_attention(qkv, batch, seq):
    blk = lambda off: pl.BlockSpec((None, seq, LANES), lambda b, t: (b, 0, off + t))
    return pl.pallas_call(
        functools.partial(_sb_kernel, n_blocks=seq // SB_Q),
        grid=(batch, N_HEAD_TILES),
        in_specs=[blk(0), blk(N_HEAD_TILES), blk(2 * N_HEAD_TILES),
                  _const_spec((2 * Q_BLOCK, 2 * Q_BLOCK))],
        out_specs=pl.BlockSpec((None, seq, LANES), lambda b, t: (b, 0, t)),
        out_shape=jax.ShapeDtypeStruct((batch, seq, ATTN_WIDTH), BF16),
        scratch_shapes=[pltpu.VMEM((seq // Q_BLOCK, 2 * Q_BLOCK, LANES), BF16),
                        pltpu.VMEM((seq // Q_BLOCK, 2 * Q_BLOCK, LANES), BF16),
                        pltpu.VMEM((SB_Q, (SB_Q // Q_BLOCK + SB_EXTRA) * 2 * Q_BLOCK), F32),
                        pltpu.VMEM((SB_Q, (SB_Q // Q_BLOCK + SB_EXTRA) * 2 * Q_BLOCK), F32),
                        pltpu.VMEM((seq, LANES), F32),
                        pltpu.VMEM((HEADS_PER_TILE, seq, Q_BLOCK), F32),
                        pltpu.SMEM((seq // SB_Q,), F32)],
        compiler_params=_params(2),
        name="sb_attention",
    )(qkv, qkv, qkv, _cumsum_weights())


def _fox_cum_kernel(f_ref, b_ref, tri_ref, o_ref, *, n_blocks):
    x = f_ref[...] + b_ref[...]
    lf = jnp.minimum(x, 0.0) - jnp.log1p(jnp.exp(-jnp.abs(x)))
    lft = lf.T
    carry = jnp.zeros((LANES, 1), F32)
    for c in range(n_blocks):
        blk = lft[:, c * Q_BLOCK:(c + 1) * Q_BLOCK]
        hi = blk.astype(BF16)
        r1 = blk - hi.astype(F32)
        mid = r1.astype(BF16)
        lo = (r1 - mid.astype(F32)).astype(BF16)
        cs = _dot(hi, tri_ref[...]) + _dot(mid, tri_ref[...]) + _dot(lo, tri_ref[...]) + carry
        o_ref[:, c * Q_BLOCK:(c + 1) * Q_BLOCK] = cs[:N_HEADS]
        carry = cs[:, Q_BLOCK - 1:Q_BLOCK]


def _fox_cum(flog, b_pad, batch, seq):
    a = np.arange(Q_BLOCK)
    tri = jnp.asarray((a[:, None] <= a[None, :]).astype(np.float32), BF16)
    return pl.pallas_call(
        functools.partial(_fox_cum_kernel, n_blocks=seq // Q_BLOCK),
        grid=(batch,),
        in_specs=[pl.BlockSpec((None, seq, LANES), lambda b: (b, 0, 0)),
                  _const_spec((1, LANES)), _const_spec((Q_BLOCK, Q_BLOCK))],
        out_specs=pl.BlockSpec((None, N_HEADS, seq), lambda b: (b, 0, 0)),
        out_shape=jax.ShapeDtypeStruct((batch, N_HEADS, seq), F32),
        compiler_params=_params(1),
        name="fox_cum",
    )(flog, b_pad, tri)


def _fox_kernel(q_ref, k_ref, v_ref, c_ref, o_ref, ks_ref, vs_ref, sa_ref, sb_ref, acc_ref, l_ref, m_ref,
                *, n_blocks):
    sel_k = _head_masks(FOX_K)
    k_per_q = FOX_Q // FOX_K

    def stack_heads(j, _):
        rows = pl.ds(pl.multiple_of(j * FOX_K, FOX_K), FOX_K)
        kj, vj = k_ref[rows, :], v_ref[rows, :]
        ks_ref[j] = jnp.concatenate([kj * sel_k[0], kj * sel_k[1]], axis=0)
        vs_ref[j] = jnp.concatenate(
            [jnp.concatenate([vj * sel_k[0], sel_k[0]], axis=1),
             jnp.concatenate([vj * sel_k[1], sel_k[1]], axis=1)], axis=0)
        return 0

    lax.fori_loop(0, n_blocks * k_per_q, stack_heads, 0)

    def scores(q_off, j, r0):
        q = q_ref[pl.ds(q_off + r0, FOX_Q - r0), :] * jnp.asarray(SCALE, BF16)
        return _dot_t(q, ks_ref[j])

    def consume(s2, q_off, j, r0, masked):
        rows = slice(r0, FOX_Q)
        n_rows = FOX_Q - r0
        head0 = lax.broadcasted_iota(jnp.int32, (n_rows, LANES), 1) < HEAD_DIM
        k_off = j * FOX_K
        ps, alphas = [], []
        for h in range(HEADS_PER_TILE):
            cj = c_ref[h, :, pl.ds(k_off, FOX_K)]
            s = s2[:, h * FOX_K:(h + 1) * FOX_K] - cj
            if masked:
                col_minus_row = (lax.broadcasted_iota(jnp.int32, (n_rows, FOX_K), 1)
                                 - lax.broadcasted_iota(jnp.int32, (n_rows, FOX_K), 0))
                s = jnp.where(col_minus_row <= q_off + r0 - k_off, s, NEG_INF)
            m_old = m_ref[h, rows]
            m_blk = jnp.max(jnp.maximum(s[:, :LANES], s[:, LANES:]), axis=-1, keepdims=True)
            m_new = jnp.maximum(m_old, jnp.broadcast_to(m_blk, (n_rows, LANES)))
            m_ref[h, rows] = m_new
            alphas.append(jnp.exp(m_old - m_new))
            ps.append(jnp.concatenate([jnp.exp(s[:, :LANES] - m_new),
                                       jnp.exp(s[:, LANES:] - m_new)], axis=1).astype(BF16))
        alpha = jnp.where(head0, alphas[0], alphas[1])
        pv = _dot(jnp.concatenate(ps, axis=1), vs_ref[j])
        acc_ref[rows] = acc_ref[rows] * alpha + pv[:, :LANES]
        l_ref[rows] = l_ref[rows] * alpha + pv[:, LANES:]

    def q_block(i):
        q_off = i * FOX_Q
        acc_ref[...] = jnp.zeros_like(acc_ref)
        l_ref[...] = jnp.zeros_like(l_ref)
        m_ref[...] = jnp.full_like(m_ref, NEG_INF)
        sa_ref[...] = scores(q_off, 0, 0)

        for p in range(i):
            j = 2 * p
            s2 = sa_ref[...]
            sb_ref[...] = scores(q_off, j + 1, 0)
            consume(s2, q_off, j, 0, False)
            s2 = sb_ref[...]
            sa_ref[...] = scores(q_off, j + 2, 0)
            consume(s2, q_off, j + 1, 0, False)
        j_diag = i * k_per_q
        consume(sa_ref[...], q_off, j_diag, 0, True)
        for d in range(1, k_per_q):
            consume(scores(q_off, j_diag + d, d * FOX_K), q_off, j_diag + d, d * FOX_K, True)
        o_ref[pl.ds(q_off, FOX_Q), :] = (acc_ref[...] / l_ref[...]).astype(BF16)

    for i in range(n_blocks):
        q_block(i)


def _fox_attention(qkv, cum, batch, seq):
    blk = lambda off: pl.BlockSpec((None, seq, LANES), lambda b, t: (b, 0, off + t))
    return pl.pallas_call(
        functools.partial(_fox_kernel, n_blocks=seq // FOX_Q),
        grid=(batch, N_HEAD_TILES),
        in_specs=[blk(0), blk(N_HEAD_TILES), blk(2 * N_HEAD_TILES),
                  pl.BlockSpec((None, HEADS_PER_TILE, 1, seq), lambda b, t: (b, t, 0, 0))],
        out_specs=pl.BlockSpec((None, seq, LANES), lambda b, t: (b, 0, t)),
        out_shape=jax.ShapeDtypeStruct((batch, seq, ATTN_WIDTH), BF16),
        scratch_shapes=[pltpu.VMEM((seq // FOX_K, 2 * FOX_K, LANES), BF16),
                        pltpu.VMEM((seq // FOX_K, 2 * FOX_K, 2 * LANES), BF16),
                        pltpu.VMEM((FOX_Q, 2 * FOX_K), F32),
                        pltpu.VMEM((FOX_Q, 2 * FOX_K), F32),
                        pltpu.VMEM((FOX_Q, LANES), F32),
                        pltpu.VMEM((FOX_Q, LANES), F32),
                        pltpu.VMEM((HEADS_PER_TILE, FOX_Q, LANES), F32)],
        compiler_params=_params(2),
        name="fox_attention",
    )(qkv, qkv, qkv, cum)


def _swa_kernel(sink_ref, q_ref, k_ref, v_ref, o_ref, ks_ref, vs_ref, *, n_blocks):
    g = pl.program_id(1)
    window = 2 * Q_BLOCK
    sel = _head_masks()
    lane = lax.broadcasted_iota(jnp.int32, (Q_BLOCK, LANES), 1)
    mine = (lane >= g * HEAD_DIM) & (lane < (g + 1) * HEAD_DIM)

    def stack_heads(j, _):
        rows = pl.ds(pl.multiple_of(j * Q_BLOCK, Q_BLOCK), Q_BLOCK)
        kg = jnp.where(mine, k_ref[rows, :].astype(F32), 0.0)
        vg = jnp.where(mine, v_ref[rows, :].astype(F32), 0.0)
        k2 = (kg + pltpu.roll(kg, HEAD_DIM, 1)).astype(BF16)
        v2 = (vg + pltpu.roll(vg, HEAD_DIM, 1)).astype(BF16)
        for h in range(HEADS_PER_TILE):
            ks_ref[h, j] = k2 * sel[h]
            vs_ref[h, j] = jnp.concatenate([v2 * sel[h], sel[h]], axis=1)
        return 0

    lax.fori_loop(0, n_blocks, stack_heads, 0)

    a = lax.broadcasted_iota(jnp.int32, (Q_BLOCK, window), 0)
    c = lax.broadcasted_iota(jnp.int32, (Q_BLOCK, window), 1)
    in_window = (c > a) & (c <= a + SWA_WINDOW)
    head0 = lane < HEAD_DIM

    def q_block(n):
        rows = pl.ds(n * Q_BLOCK, Q_BLOCK)
        prev = max(n - 1, 0)
        mask = in_window & (c + n * Q_BLOCK >= Q_BLOCK)
        kstack = jnp.concatenate([ks_ref[0, prev], ks_ref[0, n], ks_ref[1, prev], ks_ref[1, n]], axis=0)
        vext = jnp.concatenate([vs_ref[0, prev], vs_ref[0, n], vs_ref[1, prev], vs_ref[1, n]], axis=0)
        for t in range(SWA_GROUP // HEADS_PER_TILE):
            qt = q_ref[rows, t * LANES:(t + 1) * LANES] * jnp.asarray(SCALE, BF16)
            s2 = _dot_t(qt, kstack)
            es, sink_terms = [], []
            for h in range(HEADS_PER_TILE):
                s = jnp.where(mask, s2[:, h * window:(h + 1) * window], NEG_INF)
                sink = sink_ref[g * SWA_GROUP + t * HEADS_PER_TILE + h]
                m = jnp.max(jnp.maximum(s[:, :LANES], s[:, LANES:]), axis=-1, keepdims=True)
                m = jnp.maximum(jnp.broadcast_to(m, (Q_BLOCK, LANES)), sink)
                es.append(jnp.concatenate([jnp.exp(s[:, :LANES] - m),
                                           jnp.exp(s[:, LANES:] - m)], axis=1).astype(BF16))
                sink_terms.append(jnp.exp(sink - m))
            pv = _dot(jnp.concatenate(es, axis=1), vext)
            den = pv[:, LANES:] + jnp.where(head0, sink_terms[0], sink_terms[1])
            o_ref[rows, t * LANES:(t + 1) * LANES] = (pv[:, :LANES] / den).astype(BF16)

    for n in range(n_blocks):
        q_block(n)


def _swa_attention(qkv, sinks, batch, seq):
    nb = seq // Q_BLOCK
    gw = SWA_GROUP * HEAD_DIM
    k_tile = ATTN_WIDTH // LANES
    kv = lambda tile: pl.BlockSpec((None, seq, LANES), lambda b, g, s: (b, 0, tile))
    grid_spec = pltpu.PrefetchScalarGridSpec(
        num_scalar_prefetch=1,
        grid=(batch, SWA_KV_HEADS),
        in_specs=[pl.BlockSpec((None, seq, gw), lambda b, g, s: (b, 0, g)), kv(k_tile), kv(k_tile + 1)],
        out_specs=pl.BlockSpec((None, seq, gw), lambda b, g, s: (b, 0, g)),
        scratch_shapes=[pltpu.VMEM((HEADS_PER_TILE, nb, Q_BLOCK, LANES), BF16),
                        pltpu.VMEM((HEADS_PER_TILE, nb, Q_BLOCK, 2 * LANES), BF16)],
    )
    return pl.pallas_call(
        functools.partial(_swa_kernel, n_blocks=nb),
        grid_spec=grid_spec,
        out_shape=jax.ShapeDtypeStruct((batch, seq, ATTN_WIDTH), BF16),
        compiler_params=_params(2),
        name="swa_attention",
    )(sinks, qkv, qkv, qkv)


def _tail_kernel(x_ref, a_ref, p_ref, wo_ref, mg_ref, wu_ref, wd_ref, pg_ref, wg_ref, wp_ref,
                 fg_ref, o_ref, h_ref, *, final):
    x1 = x_ref[...] + _dot(a_ref[...], wo_ref[...].astype(BF16))
    h_ref[...] = _rms(x1, mg_ref[...]).astype(BF16)
    x2 = x1
    for c in range(0, D_FF, FF_CHUNK):
        u = jnp.maximum(_dot(h_ref[...], wu_ref[:, c:c + FF_CHUNK].astype(BF16)), 0.0)
        x2 = x2 + _dot((u * u).astype(BF16), wd_ref[c:c + FF_CHUNK, :])
    h3 = _rms(x2, pg_ref[...]).astype(BF16)
    gate = 1.0 / (1.0 + jnp.exp(-_dot(h3, wg_ref[...].astype(BF16))))
    x3 = x2 + _dot(p_ref[...].astype(BF16), wp_ref[...].astype(BF16)) * gate
    o_ref[...] = _rms(x3, fg_ref[...]) if final else x3


def _tail(x2d, a2d, p3d, layer, wo, mg, wu, wd, pg, wg, wp, fg, final):
    t, d = x2d.shape
    tok = lambda w: pl.BlockSpec((TOKEN_TILE, w), lambda i: (i, 0))
    return pl.pallas_call(
        functools.partial(_tail_kernel, final=final),
        grid=(t // TOKEN_TILE,),
        in_specs=[tok(d), tok(ATTN_WIDTH),
                  pl.BlockSpec((None, TOKEN_TILE, PLE_DIM), lambda i: (layer, i, 0)),
                  _const_spec((ATTN_WIDTH, d)), _const_spec((1, d)),
                  _const_spec((d, D_FF)), _const_spec((D_FF, d)), _const_spec((1, d)),
                  _const_spec((d, d)), _const_spec((PLE_DIM, d)), _const_spec((1, d))],
        out_specs=tok(d),
        out_shape=jax.ShapeDtypeStruct((t, d), F32),
        scratch_shapes=[pltpu.VMEM((TOKEN_TILE, d), BF16)],
        compiler_params=_params(1),
        name="layer_tail",
    )(x2d, a2d, p3d, wo, mg, wu, wd, pg, wg, wp, fg)


def kernel(x, p, attn_norm_0, w_in_0, w_out_0, mlp_norm_0, w_up_0, w_down_0, ple_norm_0, w_ple_gate_0, w_ple_proj_0, attn_norm_1, w_in_1, w_out_1, sinks_1, mlp_norm_1, w_up_1, w_down_1, ple_norm_1, w_ple_gate_1, w_ple_proj_1, attn_norm_2, w_in_2, w_out_2, b_forget_2, mlp_norm_2, w_up_2, w_down_2, ple_norm_2, w_ple_gate_2, w_ple_proj_2, attn_norm_3, w_in_3, w_out_3, mlp_norm_3, w_up_3, w_down_3, ple_norm_3, w_ple_gate_3, w_ple_proj_3, final_norm):
    batch, seq, d = x.shape
    t = batch * seq
    layers = [
        (attn_norm_0, w_in_0, w_out_0, None, mlp_norm_0, w_up_0, w_down_0, ple_norm_0, w_ple_gate_0, w_ple_proj_0),
        (attn_norm_1, w_in_1, w_out_1, sinks_1, mlp_norm_1, w_up_1, w_down_1, ple_norm_1, w_ple_gate_1, w_ple_proj_1),
        (attn_norm_2, w_in_2, w_out_2, b_forget_2, mlp_norm_2, w_up_2, w_down_2, ple_norm_2, w_ple_gate_2, w_ple_proj_2),
        (attn_norm_3, w_in_3, w_out_3, None, mlp_norm_3, w_up_3, w_down_3, ple_norm_3, w_ple_gate_3, w_ple_proj_3),
    ]
    row = lambda g: g.reshape(1, d)
    bf = lambda w: w.astype(BF16)
    x2d = x.reshape(t, d)
    p3d = p.reshape(p.shape[0], t, PLE_DIM)
    fg = row(final_norm)
    for i, (an, wi, wo, extra, mn, wu, wd, pn, wg, wp) in enumerate(layers):
        kind = i % 3
        if kind == 0:
            qkv = _inproj(x2d, row(an), wi)
            a = _sb_attention(qkv.reshape(batch, seq, -1), batch, seq)
        elif kind == 1:
            cos, s1, s2 = _rope_tables(seq)
            qkv = _inproj_swa(x2d, row(an), wi, cos, s1, s2, seq)
            a = _swa_attention(qkv.reshape(batch, seq, -1), extra.astype(F32), batch, seq)
        else:
            wf = jnp.pad(wi[:, 3 * ATTN_WIDTH:], ((0, 0), (0, LANES - N_HEADS)))
            qkv, flog = _inproj_fox(x2d, row(an), wi, bf(wf))
            b_pad = jnp.pad(extra.astype(F32), (0, LANES - N_HEADS)).reshape(1, LANES)
            cum = _fox_cum(flog.reshape(batch, seq, LANES), b_pad, batch, seq)
            a = _fox_attention(qkv.reshape(batch, seq, -1),
                               cum.reshape(batch, N_HEADS, 1, seq), batch, seq)
        x2d = _tail(x2d, a.reshape(t, ATTN_WIDTH), p3d, i,
                    wo, row(mn), wu, bf(wd), row(pn), wg, wp, fg,
                    final=(i == len(layers) - 1))
    return x2d.reshape(batch, seq, d)
```

```python
import functools

import jax
import jax.numpy as jnp
import numpy as np
from jax import lax
from jax.experimental import pallas as pl
from jax.experimental.pallas import tpu as pltpu

D_MODEL = 1024
HEAD_DIM = 64
N_HEADS = 16
ATTN_WIDTH = N_HEADS * HEAD_DIM
SWA_KV_HEADS = 2
SWA_GROUP = N_HEADS // SWA_KV_HEADS
SWA_WINDOW = 128
ROPE_THETA = 500000.0
ROPE_DIM = 16
D_FF = 4 * D_MODEL
PLE_DIM = 256
Q_BLOCK = 128
RMS_EPS = 1e-6
NEG_INF = -1e30
SCALE = HEAD_DIM ** -0.5

LANES = 128
HEADS_PER_TILE = LANES // HEAD_DIM
N_HEAD_TILES = ATTN_WIDTH // LANES
TOKEN_TILE = 512
FF_CHUNK = 1024
VMEM_LIMIT = 56 * 1024 * 1024

SB_Q = 2 * Q_BLOCK
SB_EXTRA = 2
SB_TILES = 2
SB_EXIT = 104.0
LOG2E = 1.4426950408889634
FOX_Q = 512
FOX_K = 256

assert FOX_K == 2 * LANES and FOX_Q == 2 * FOX_K
assert SB_Q == 2 * Q_BLOCK and SB_EXTRA <= SB_Q // Q_BLOCK

F32 = jnp.float32
BF16 = jnp.bfloat16


def _params(n_axes):
    return pltpu.CompilerParams(dimension_semantics=("parallel",) * n_axes,
                                vmem_limit_bytes=VMEM_LIMIT)


def _const_spec(shape):
    nd = len(shape)
    return pl.BlockSpec(shape, lambda *_: (0,) * nd, pipeline_mode=pl.Buffered(1))


def _rms(x, g):
    ms = jnp.mean(x * x, axis=-1, keepdims=True)
    return (x * lax.rsqrt(ms + RMS_EPS)) * g


def _dot(a, b):
    return jnp.dot(a, b, preferred_element_type=F32)


def _dot_t(a, b):
    return lax.dot_general(a, b, (((1,), (1,)), ((), ())), preferred_element_type=F32)


def _head_masks(rows=Q_BLOCK):
    lane = lax.broadcasted_iota(jnp.int32, (rows, LANES), 1)
    m0 = jnp.where(lane < HEAD_DIM, 1.0, 0.0).astype(BF16)
    m1 = jnp.where(lane < HEAD_DIM, 0.0, 1.0).astype(BF16)
    return m0, m1


def _inproj_kernel(x_ref, g_ref, w_ref, o_ref, h_ref, *, n_out):
    h_ref[...] = _rms(x_ref[...], g_ref[...]).astype(BF16)
    for c in range(0, n_out, FF_CHUNK):
        o_ref[:, c:c + FF_CHUNK] = _dot(h_ref[...], w_ref[:, c:c + FF_CHUNK].astype(BF16)).astype(BF16)


def _inproj(x2d, g, w):
    t, d = x2d.shape
    n_out = w.shape[1]
    return pl.pallas_call(
        functools.partial(_inproj_kernel, n_out=n_out),
        grid=(t // TOKEN_TILE,),
        in_specs=[pl.BlockSpec((TOKEN_TILE, d), lambda i: (i, 0)),
                  _const_spec((1, d)), _const_spec((d, n_out))],
        out_specs=pl.BlockSpec((TOKEN_TILE, n_out), lambda i: (i, 0)),
        out_shape=jax.ShapeDtypeStruct((t, n_out), BF16),
        scratch_shapes=[pltpu.VMEM((TOKEN_TILE, d), BF16)],
        compiler_params=_params(1),
        name="inproj",
    )(x2d, g, w)


def _inproj_fox_kernel(x_ref, g_ref, w_ref, wf_ref, o_ref, f_ref, h_ref, *, n_out):
    h_ref[...] = _rms(x_ref[...], g_ref[...]).astype(BF16)
    for c in range(0, n_out, FF_CHUNK):
        o_ref[:, c:c + FF_CHUNK] = _dot(h_ref[...], w_ref[:, c:c + FF_CHUNK].astype(BF16)).astype(BF16)
    f_ref[...] = _dot(h_ref[...], wf_ref[...])


def _inproj_fox(x2d, g, w, wf):
    t, d = x2d.shape
    n_out = 3 * ATTN_WIDTH
    return pl.pallas_call(
        functools.partial(_inproj_fox_kernel, n_out=n_out),
        grid=(t // TOKEN_TILE,),
        in_specs=[pl.BlockSpec((TOKEN_TILE, d), lambda i: (i, 0)),
                  _const_spec((1, d)), _const_spec(w.shape), _const_spec((d, LANES))],
        out_specs=[pl.BlockSpec((TOKEN_TILE, n_out), lambda i: (i, 0)),
                   pl.BlockSpec((TOKEN_TILE, LANES), lambda i: (i, 0))],
        out_shape=[jax.ShapeDtypeStruct((t, n_out), BF16),
                   jax.ShapeDtypeStruct((t, LANES), F32)],
        scratch_shapes=[pltpu.VMEM((TOKEN_TILE, d), BF16)],
        compiler_params=_params(1),
        name="inproj_fox",
    )(x2d, g, w, wf)


def _inproj_swa_kernel(x_ref, g_ref, w_ref, c_ref, s1_ref, s2_ref, o_ref, h_ref, *, n_rope, n_out):
    h_ref[...] = _rms(x_ref[...], g_ref[...]).astype(BF16)
    cos, s1, s2 = c_ref[...], s1_ref[...], s2_ref[...]
    for c in range(0, n_out, LANES):
        y = _dot(h_ref[...], w_ref[:, c:c + LANES].astype(BF16))
        if c < n_rope:
            y = y * cos + pltpu.roll(y, 8, 1) * s1 + pltpu.roll(y, LANES - 8, 1) * s2
        o_ref[:, c:c + LANES] = y.astype(BF16)


def _inproj_swa(x2d, g, w, cos, s1, s2, seq):
    t, d = x2d.shape
    n_out = w.shape[1]
    n_rope = ATTN_WIDTH + SWA_KV_HEADS * HEAD_DIM
    tiles_per_seq = seq // TOKEN_TILE
    tab = pl.BlockSpec((TOKEN_TILE, LANES), lambda i: (i % tiles_per_seq, 0))
    return pl.pallas_call(
        functools.partial(_inproj_swa_kernel, n_rope=n_rope, n_out=n_out),
        grid=(t // TOKEN_TILE,),
        in_specs=[pl.BlockSpec((TOKEN_TILE, d), lambda i: (i, 0)),
                  _const_spec((1, d)), _const_spec((d, n_out)), tab, tab, tab],
        out_specs=pl.BlockSpec((TOKEN_TILE, n_out), lambda i: (i, 0)),
        out_shape=jax.ShapeDtypeStruct((t, n_out), BF16),
        scratch_shapes=[pltpu.VMEM((TOKEN_TILE, d), BF16)],
        compiler_params=_params(1),
        name="inproj_swa",
    )(x2d, g, w, cos, s1, s2)


def _rope_tables(seq):
    half = ROPE_DIM // 2
    inv_freq = ROPE_THETA ** (-jnp.arange(half, dtype=F32) / half)
    ang = jnp.arange(seq, dtype=jnp.int32).astype(F32)[:, None] * inv_freq[None, :]
    cos, sin = jnp.cos(ang), jnp.sin(ang)
    one = jnp.ones((seq, HEAD_DIM - ROPE_DIM), F32)
    zero = jnp.zeros((seq, HEAD_DIM - ROPE_DIM), F32)
    zh = jnp.zeros((seq, half), F32)
    c_head = jnp.concatenate([cos, cos, one], axis=1)
    s1_head = jnp.concatenate([zh, sin, zero], axis=1)
    s2_head = jnp.concatenate([-sin, zh, zero], axis=1)
    tile = lambda a: jnp.concatenate([a] * HEADS_PER_TILE, axis=1)
    return tile(c_head), tile(s1_head), tile(s2_head)


def _sb_tile(t, q_ref, k_ref, v_ref, uu_ref, o_ref, ks_ref, vs_ref, za_ref, zb_ref, acc_ref, r_ref,
             rmin_ref, n_blocks):
    sel = _head_masks()
    blocks_per_q = SB_Q // Q_BLOCK
    n_static = blocks_per_q + SB_EXTRA
    scale = jnp.asarray(SCALE, BF16)
    lanes = slice(t * LANES, (t + 1) * LANES)

    def stack_heads(j):
        rows = pl.ds(pl.multiple_of(j * Q_BLOCK, Q_BLOCK), Q_BLOCK)
        kj, vj = k_ref[rows, lanes], v_ref[rows, lanes]
        ks_ref[j] = jnp.concatenate([kj * sel[0], kj * sel[1]], axis=0)
        vs_ref[j] = jnp.concatenate([vj * sel[0], vj * sel[1]], axis=0)

    def strict(n_rows):
        return (lax.broadcasted_iota(jnp.int32, (n_rows, Q_BLOCK), 1)
                < lax.broadcasted_iota(jnp.int32, (n_rows, Q_BLOCK), 0))

    def scores_ahead(i):
        q = q_ref[pl.ds(i * SB_Q, SB_Q), lanes] * scale
        ks = ks_ref[pl.ds(i * blocks_per_q - SB_EXTRA, n_static)]
        return _dot_t(q, ks.reshape(n_static * 2 * Q_BLOCK, LANES))

    def key_block(z2, r_in, keep):
        ws, r_out = [], []
        for h in range(HEADS_PER_TILE):
            z = z2[:, h * Q_BLOCK:(h + 1) * Q_BLOCK]
            sp = jnp.maximum(z, 0.0) + jnp.log(1.0 + jnp.exp2(jnp.abs(z) * -LOG2E))
            spm = sp if keep is None else jnp.where(keep, sp, 0.0)
            hi = spm.astype(BF16)
            lo = (spm - hi.astype(F32)).astype(BF16)
            cs = _dot(jnp.concatenate([hi, lo], axis=1), uu_ref[...])
            tail, tot = cs[:, :Q_BLOCK], cs[:, Q_BLOCK:]
            if r_in is not None:
                tail, tot = tail + r_in[h], tot + r_in[h]
            w = jnp.exp2((z - tail) * LOG2E)
            if keep is not None:
                w = jnp.where(keep, w, 0.0)
            ws.append(w.astype(BF16))
            r_out.append(tot)
        return jnp.concatenate(ws, axis=1), r_out

    def first_blocks(i, n_extra, z_ref):
        q_off = i * SB_Q
        j_diag = i * blocks_per_q

        def z_of(d, r0=0):
            if z_ref is None:
                return _dot_t(q_ref[pl.ds(q_off + r0, SB_Q - r0), lanes] * scale, ks_ref[j_diag - d])
            c0 = (SB_EXTRA - d) * 2 * Q_BLOCK
            return z_ref[r0:, c0:c0 + 2 * Q_BLOCK]

        zeros = jnp.zeros((Q_BLOCK, LANES), F32)
        w_lo, r_lo = key_block(z_of(-1, Q_BLOCK), None, strict(Q_BLOCK))
        r = [jnp.concatenate([zeros, r_lo[h]], axis=0) for h in range(HEADS_PER_TILE)]
        w_diag, r = key_block(z_of(0), r, strict(SB_Q))
        ws = [w_diag, jnp.concatenate([jnp.zeros_like(w_lo), w_lo], axis=0)]
        for d in range(1, n_extra + 1):
            w_d, r = key_block(z_of(d), r, None)
            ws.insert(0, w_d)
        vs = vs_ref[pl.ds(j_diag - n_extra, n_extra + blocks_per_q)]
        acc = _dot(jnp.concatenate(ws, axis=1), vs.reshape((n_extra + blocks_per_q) * 2 * Q_BLOCK, LANES))
        acc_ref[pl.ds(q_off, SB_Q), :] = acc
        for h in range(HEADS_PER_TILE):
            r_ref[h, pl.ds(q_off, SB_Q), :] = r[h]
        rmin_ref[t * n_blocks + i] = jnp.min(jnp.minimum(r[0], r[1]))

    def remaining_blocks(i):
        rows = pl.ds(pl.multiple_of(i * SB_Q, SB_Q), SB_Q)
        q = q_ref[rows, lanes] * scale

        def more(c):
            j, r_min = c
            return (j >= 0) & (r_min <= SB_EXIT)

        def step(c):
            j, _ = c
            w, r_new = key_block(_dot_t(q, ks_ref[j]),
                                 [r_ref[h, rows, :] for h in range(HEADS_PER_TILE)], None)
            acc_ref[rows, :] += _dot(w, vs_ref[j])
            for h in range(HEADS_PER_TILE):
                r_ref[h, rows, :] = r_new[h]
            return j - 1, jnp.min(jnp.minimum(r_new[0], r_new[1]))

        lax.while_loop(more, step, (i * blocks_per_q - SB_EXTRA - 1, rmin_ref[t * n_blocks + i]))
        o_ref[rows, lanes] = acc_ref[rows, :].astype(BF16)

    def block(i):
        z_refs = (za_ref, zb_ref)
        if i + 1 < n_blocks:
            z_refs[(i + 1) % 2][...] = scores_ahead(i + 1)
        first_blocks(i, SB_EXTRA if i else 0, z_refs[i % 2] if i else None)

    def write_first():
        o_ref[pl.ds(0, SB_Q), lanes] = acc_ref[pl.ds(0, SB_Q), :].astype(BF16)

    return stack_heads, block, write_first, remaining_blocks


def _sb_kernel(q_ref, k_ref, v_ref, uu_ref, o_ref, ks_ref, vs_ref, za_ref, zb_ref, acc_ref, r_ref,
               rmin_ref, *, n_blocks):
    tiles = [_sb_tile(t, q_ref, k_ref, v_ref, uu_ref, o_ref, ks_ref.at[t], vs_ref.at[t],
                      za_ref.at[t], zb_ref.at[t], acc_ref.at[t], r_ref.at[t], rmin_ref, n_blocks)
             for t in range(SB_TILES)]

    def stack_all(j, _):
        for stack_heads, _, _, _ in tiles:
            stack_heads(j)
        return 0

    lax.fori_loop(0, n_blocks * (SB_Q // Q_BLOCK), stack_all, 0)
    for i in range(n_blocks):
        for _, block, _, _ in tiles:
            block(i)
    for _, _, write_first, _ in tiles:
        write_first()

    def remaining_all(i, _):
        for _, _, _, remaining_blocks in tiles:
            remaining_blocks(i)
        return 0

    lax.fori_loop(1, n_blocks, remaining_all, 0)


def _cumsum_weights():
    a = np.arange(Q_BLOCK)
    u = (a[:, None] >= a[None, :]).astype(np.float32)
    half = np.concatenate([u, np.ones_like(u)], axis=1)
    return jnp.asarray(np.concatenate([half, half], axis=0), BF16)


def _sb_attention(qkv, batch, seq):
    w = SB_TILES * LANES
    n_groups = ATTN_WIDTH // w
    blk = lambda off: pl.BlockSpec((None, seq, w), lambda b, t: (b, 0, off + t))
    z_shape = (SB_TILES, SB_Q, (SB_Q // Q_BLOCK + SB_EXTRA) * 2 * Q_BLOCK)
    return pl.pallas_call(
        functools.partial(_sb_kernel, n_blocks=seq // SB_Q),
        grid=(batch, n_groups),
        in_specs=[blk(0), blk(n_groups), blk(2 * n_groups),
                  _const_spec((2 * Q_BLOCK, 2 * Q_BLOCK))],
        out_specs=pl.BlockSpec((None, seq, w), lambda b, t: (b, 0, t)),
        out_shape=jax.ShapeDtypeStruct((batch, seq, ATTN_WIDTH), BF16),
        scratch_shapes=[pltpu.VMEM((SB_TILES, seq // Q_BLOCK, 2 * Q_BLOCK, LANES), BF16),
                        pltpu.VMEM((SB_TILES, seq // Q_BLOCK, 2 * Q_BLOCK, LANES), BF16),
                        pltpu.VMEM(z_shape, F32),
                        pltpu.VMEM(z_shape, F32),
                        pltpu.VMEM((SB_TILES, seq, LANES), F32),
                        pltpu.VMEM((SB_TILES, HEADS_PER_TILE, seq, Q_BLOCK), F32),
                        pltpu.SMEM((SB_TILES * (seq // SB_Q),), F32)],
        compiler_params=_params(2),
        name="sb_attention",
    )(qkv, qkv, qkv, _cumsum_weights())


def _fox_cum_kernel(f_ref, b_ref, tri_ref, o_ref, *, n_blocks):
    x = f_ref[...] + b_ref[...]
    lf = jnp.minimum(x, 0.0) - jnp.log1p(jnp.exp(-jnp.abs(x)))
    lft = lf.T
    carry = jnp.zeros((LANES, 1), F32)
    for c in range(n_blocks):
        blk = lft[:, c * Q_BLOCK:(c + 1) * Q_BLOCK]
        hi = blk.astype(BF16)
        r1 = blk - hi.astype(F32)
        mid = r1.astype(BF16)
        lo = (r1 - mid.astype(F32)).astype(BF16)
        cs = _dot(hi, tri_ref[...]) + _dot(mid, tri_ref[...]) + _dot(lo, tri_ref[...]) + carry
        o_ref[:, c * Q_BLOCK:(c + 1) * Q_BLOCK] = cs[:N_HEADS]
        carry = cs[:, Q_BLOCK - 1:Q_BLOCK]


def _fox_cum(flog, b_pad, batch, seq):
    a = np.arange(Q_BLOCK)
    tri = jnp.asarray((a[:, None] <= a[None, :]).astype(np.float32), BF16)
    return pl.pallas_call(
        functools.partial(_fox_cum_kernel, n_blocks=seq // Q_BLOCK),
        grid=(batch,),
        in_specs=[pl.BlockSpec((None, seq, LANES), lambda b: (b, 0, 0)),
                  _const_spec((1, LANES)), _const_spec((Q_BLOCK, Q_BLOCK))],
        out_specs=pl.BlockSpec((None, N_HEADS, seq), lambda b: (b, 0, 0)),
        out_shape=jax.ShapeDtypeStruct((batch, N_HEADS, seq), F32),
        compiler_params=_params(1),
        name="fox_cum",
    )(flog, b_pad, tri)


def _fox_kernel(q_ref, k_ref, v_ref, c_ref, o_ref, ks_ref, vs_ref, sa_ref, sb_ref, acc_ref, l_ref, m_ref,
                *, n_blocks):
    sel_k = _head_masks(FOX_K)
    k_per_q = FOX_Q // FOX_K

    def stack_heads(j, _):
        rows = pl.ds(pl.multiple_of(j * FOX_K, FOX_K), FOX_K)
        kj, vj = k_ref[rows, :], v_ref[rows, :]
        ks_ref[j] = jnp.concatenate([kj * sel_k[0], kj * sel_k[1]], axis=0)
        vs_ref[j] = jnp.concatenate(
            [jnp.concatenate([vj * sel_k[0], sel_k[0]], axis=1),
             jnp.concatenate([vj * sel_k[1], sel_k[1]], axis=1)], axis=0)
        return 0

    lax.fori_loop(0, n_blocks * k_per_q, stack_heads, 0)

    def scores(q_off, j, r0):
        q = q_ref[pl.ds(q_off + r0, FOX_Q - r0), :] * jnp.asarray(SCALE, BF16)
        return _dot_t(q, ks_ref[j])

    def consume(s2, q_off, j, r0, masked):
        rows = slice(r0, FOX_Q)
        n_rows = FOX_Q - r0
        head0 = lax.broadcasted_iota(jnp.int32, (n_rows, LANES), 1) < HEAD_DIM
        k_off = j * FOX_K
        ps, alphas = [], []
        for h in range(HEADS_PER_TILE):
            cj = c_ref[h, :, pl.ds(k_off, FOX_K)]
            s = s2[:, h * FOX_K:(h + 1) * FOX_K] - cj
            if masked:
                col_minus_row = (lax.broadcasted_iota(jnp.int32, (n_rows, FOX_K), 1)
                                 - lax.broadcasted_iota(jnp.int32, (n_rows, FOX_K), 0))
                s = jnp.where(col_minus_row <= q_off + r0 - k_off, s, NEG_INF)
            m_old = m_ref[h, rows]
            m_blk = jnp.max(jnp.maximum(s[:, :LANES], s[:, LANES:]), axis=-1, keepdims=True)
            m_new = jnp.maximum(m_old, jnp.broadcast_to(m_blk, (n_rows, LANES)))
            m_ref[h, rows] = m_new
            alphas.append(jnp.exp(m_old - m_new))
            ps.append(jnp.concatenate([jnp.exp(s[:, :LANES] - m_new),
                                       jnp.exp(s[:, LANES:] - m_new)], axis=1).astype(BF16))
        alpha = jnp.where(head0, alphas[0], alphas[1])
        pv = _dot(jnp.concatenate(ps, axis=1), vs_ref[j])
        acc_ref[rows] = acc_ref[rows] * alpha + pv[:, :LANES]
        l_ref[rows] = l_ref[rows] * alpha + pv[:, LANES:]

    def q_block(i):
        q_off = i * FOX_Q
        acc_ref[...] = jnp.zeros_like(acc_ref)
        l_ref[...] = jnp.zeros_like(l_ref)
        m_ref[...] = jnp.full_like(m_ref, NEG_INF)
        sa_ref[...] = scores(q_off, 0, 0)

        for p in range(i):
            j = 2 * p
            s2 = sa_ref[...]
            sb_ref[...] = scores(q_off, j + 1, 0)
            consume(s2, q_off, j, 0, False)
            s2 = sb_ref[...]
            sa_ref[...] = scores(q_off, j + 2, 0)
            consume(s2, q_off, j + 1, 0, False)
        j_diag = i * k_per_q
        consume(sa_ref[...], q_off, j_diag, 0, True)
        for d in range(1, k_per_q):
            consume(scores(q_off, j_diag + d, d * FOX_K), q_off, j_diag + d, d * FOX_K, True)
        o_ref[pl.ds(q_off, FOX_Q), :] = (acc_ref[...] / l_ref[...]).astype(BF16)

    for i in range(n_blocks):
        q_block(i)


def _fox_attention(qkv, cum, batch, seq):
    blk = lambda off: pl.BlockSpec((None, seq, LANES), lambda b, t: (b, 0, off + t))
    return pl.pallas_call(
        functools.partial(_fox_kernel, n_blocks=seq // FOX_Q),
        grid=(batch, N_HEAD_TILES),
        in_specs=[blk(0), blk(N_HEAD_TILES), blk(2 * N_HEAD_TILES),
                  pl.BlockSpec((None, HEADS_PER_TILE, 1, seq), lambda b, t: (b, t, 0, 0))],
        out_specs=pl.BlockSpec((None, seq, LANES), lambda b, t: (b, 0, t)),
        out_shape=jax.ShapeDtypeStruct((batch, seq, ATTN_WIDTH), BF16),
        scratch_shapes=[pltpu.VMEM((seq // FOX_K, 2 * FOX_K, LANES), BF16),
                        pltpu.VMEM((seq // FOX_K, 2 * FOX_K, 2 * LANES), BF16),
                        pltpu.VMEM((FOX_Q, 2 * FOX_K), F32),
                        pltpu.VMEM((FOX_Q, 2 * FOX_K), F32),
                        pltpu.VMEM((FOX_Q, LANES), F32),
                        pltpu.VMEM((FOX_Q, LANES), F32),
                        pltpu.VMEM((HEADS_PER_TILE, FOX_Q, LANES), F32)],
        compiler_params=_params(2),
        name="fox_attention",
    )(qkv, qkv, qkv, cum)


def _swa_kernel(sink_ref, q_ref, k_ref, v_ref, o_ref, ks_ref, vs_ref, *, n_blocks):
    g = pl.program_id(1)
    window = 2 * Q_BLOCK
    sel = _head_masks()
    lane = lax.broadcasted_iota(jnp.int32, (Q_BLOCK, LANES), 1)
    mine = (lane >= g * HEAD_DIM) & (lane < (g + 1) * HEAD_DIM)

    def stack_heads(j, _):
        rows = pl.ds(pl.multiple_of(j * Q_BLOCK, Q_BLOCK), Q_BLOCK)
        kg = jnp.where(mine, k_ref[rows, :].astype(F32), 0.0)
        vg = jnp.where(mine, v_ref[rows, :].astype(F32), 0.0)
        k2 = (kg + pltpu.roll(kg, HEAD_DIM, 1)).astype(BF16)
        v2 = (vg + pltpu.roll(vg, HEAD_DIM, 1)).astype(BF16)
        for h in range(HEADS_PER_TILE):
            ks_ref[h, j] = k2 * sel[h]
            vs_ref[h, j] = jnp.concatenate([v2 * sel[h], sel[h]], axis=1)
        return 0

    lax.fori_loop(0, n_blocks, stack_heads, 0)

    a = lax.broadcasted_iota(jnp.int32, (Q_BLOCK, window), 0)
    c = lax.broadcasted_iota(jnp.int32, (Q_BLOCK, window), 1)
    in_window = (c > a) & (c <= a + SWA_WINDOW)
    head0 = lane < HEAD_DIM

    def q_block(n):
        rows = pl.ds(n * Q_BLOCK, Q_BLOCK)
        prev = max(n - 1, 0)
        mask = in_window & (c + n * Q_BLOCK >= Q_BLOCK)
        kstack = jnp.concatenate([ks_ref[0, prev], ks_ref[0, n], ks_ref[1, prev], ks_ref[1, n]], axis=0)
        vext = jnp.concatenate([vs_ref[0, prev], vs_ref[0, n], vs_ref[1, prev], vs_ref[1, n]], axis=0)
        for t in range(SWA_GROUP // HEADS_PER_TILE):
            qt = q_ref[rows, t * LANES:(t + 1) * LANES] * jnp.asarray(SCALE, BF16)
            s2 = _dot_t(qt, kstack)
            es, sink_terms = [], []
            for h in range(HEADS_PER_TILE):
                s = jnp.where(mask, s2[:, h * window:(h + 1) * window], NEG_INF)
                sink = sink_ref[g * SWA_GROUP + t * HEADS_PER_TILE + h]
                m = jnp.max(jnp.maximum(s[:, :LANES], s[:, LANES:]), axis=-1, keepdims=True)
                m = jnp.maximum(jnp.broadcast_to(m, (Q_BLOCK, LANES)), sink)
                es.append(jnp.concatenate([jnp.exp(s[:, :LANES] - m),
                                           jnp.exp(s[:, LANES:] - m)], axis=1).astype(BF16))
                sink_terms.append(jnp.exp(sink - m))
            pv = _dot(jnp.concatenate(es, axis=1), vext)
            den = pv[:, LANES:] + jnp.where(head0, sink_terms[0], sink_terms[1])
            o_ref[rows, t * LANES:(t + 1) * LANES] = (pv[:, :LANES] / den).astype(BF16)

    for n in range(n_blocks):
        q_block(n)


def _swa_attention(qkv, sinks, batch, seq):
    nb = seq // Q_BLOCK
    gw = SWA_GROUP * HEAD_DIM
    k_tile = ATTN_WIDTH // LANES
    kv = lambda tile: pl.BlockSpec((None, seq, LANES), lambda b, g, s: (b, 0, tile))
    grid_spec = pltpu.PrefetchScalarGridSpec(
        num_scalar_prefetch=1,
        grid=(batch, SWA_KV_HEADS),
        in_specs=[pl.BlockSpec((None, seq, gw), lambda b, g, s: (b, 0, g)), kv(k_tile), kv(k_tile + 1)],
        out_specs=pl.BlockSpec((None, seq, gw), lambda b, g, s: (b, 0, g)),
        scratch_shapes=[pltpu.VMEM((HEADS_PER_TILE, nb, Q_BLOCK, LANES), BF16),
                        pltpu.VMEM((HEADS_PER_TILE, nb, Q_BLOCK, 2 * LANES), BF16)],
    )
    return pl.pallas_call(
        functools.partial(_swa_kernel, n_blocks=nb),
        grid_spec=grid_spec,
        out_shape=jax.ShapeDtypeStruct((batch, seq, ATTN_WIDTH), BF16),
        compiler_params=_params(2),
        name="swa_attention",
    )(sinks, qkv, qkv, qkv)


def _tail_kernel(x_ref, a_ref, p_ref, wo_ref, mg_ref, wu_ref, wd_ref, pg_ref, wg_ref, wp_ref,
                 fg_ref, o_ref, h_ref, *, final):
    x1 = x_ref[...] + _dot(a_ref[...], wo_ref[...].astype(BF16))
    h_ref[...] = _rms(x1, mg_ref[...]).astype(BF16)
    x2 = x1
    for c in range(0, D_FF, FF_CHUNK):
        u = jnp.maximum(_dot(h_ref[...], wu_ref[:, c:c + FF_CHUNK].astype(BF16)), 0.0)
        x2 = x2 + _dot((u * u).astype(BF16), wd_ref[c:c + FF_CHUNK, :])
    h3 = _rms(x2, pg_ref[...]).astype(BF16)
    gate = 1.0 / (1.0 + jnp.exp(-_dot(h3, wg_ref[...].astype(BF16))))
    x3 = x2 + _dot(p_ref[...].astype(BF16), wp_ref[...].astype(BF16)) * gate
    o_ref[...] = _rms(x3, fg_ref[...]) if final else x3


def _tail(x2d, a2d, p3d, layer, wo, mg, wu, wd, pg, wg, wp, fg, final):
    t, d = x2d.shape
    tok = lambda w: pl.BlockSpec((TOKEN_TILE, w), lambda i: (i, 0))
    return pl.pallas_call(
        functools.partial(_tail_kernel, final=final),
        grid=(t // TOKEN_TILE,),
        in_specs=[tok(d), tok(ATTN_WIDTH),
                  pl.BlockSpec((None, TOKEN_TILE, PLE_DIM), lambda i: (layer, i, 0)),
                  _const_spec((ATTN_WIDTH, d)), _const_spec((1, d)),
                  _const_spec((d, D_FF)), _const_spec((D_FF, d)), _const_spec((1, d)),
                  _const_spec((d, d)), _const_spec((PLE_DIM, d)), _const_spec((1, d))],
        out_specs=tok(d),
        out_shape=jax.ShapeDtypeStruct((t, d), F32),
        scratch_shapes=[pltpu.VMEM((TOKEN_TILE, d), BF16)],
        compiler_params=_params(1),
        name="layer_tail",
    )(x2d, a2d, p3d, wo, mg, wu, wd, pg, wg, wp, fg)


def kernel(x, p, attn_norm_0, w_in_0, w_out_0, mlp_norm_0, w_up_0, w_down_0, ple_norm_0, w_ple_gate_0, w_ple_proj_0, attn_norm_1, w_in_1, w_out_1, sinks_1, mlp_norm_1, w_up_1, w_down_1, ple_norm_1, w_ple_gate_1, w_ple_proj_1, attn_norm_2, w_in_2, w_out_2, b_forget_2, mlp_norm_2, w_up_2, w_down_2, ple_norm_2, w_ple_gate_2, w_ple_proj_2, attn_norm_3, w_in_3, w_out_3, mlp_norm_3, w_up_3, w_down_3, ple_norm_3, w_ple_gate_3, w_ple_proj_3, final_norm):
    batch, seq, d = x.shape
    t = batch * seq
    layers = [
        (attn_norm_0, w_in_0, w_out_0, None, mlp_norm_0, w_up_0, w_down_0, ple_norm_0, w_ple_gate_0, w_ple_proj_0),
        (attn_norm_1, w_in_1, w_out_1, sinks_1, mlp_norm_1, w_up_1, w_down_1, ple_norm_1, w_ple_gate_1, w_ple_proj_1),
        (attn_norm_2, w_in_2, w_out_2, b_forget_2, mlp_norm_2, w_up_2, w_down_2, ple_norm_2, w_ple_gate_2, w_ple_proj_2),
        (attn_norm_3, w_in_3, w_out_3, None, mlp_norm_3, w_up_3, w_down_3, ple_norm_3, w_ple_gate_3, w_ple_proj_3),
    ]
    row = lambda g: g.reshape(1, d)
    bf = lambda w: w.astype(BF16)
    x2d = x.reshape(t, d)
    p3d = p.reshape(p.shape[0], t, PLE_DIM)
    fg = row(final_norm)
    for i, (an, wi, wo, extra, mn, wu, wd, pn, wg, wp) in enumerate(layers):
        kind = i % 3
        if kind == 0:
            qkv = _inproj(x2d, row(an), wi)
            a = _sb_attention(qkv.reshape(batch, seq, -1), batch, seq)
        elif kind == 1:
            cos, s1, s2 = _rope_tables(seq)
            qkv = _inproj_swa(x2d, row(an), wi, cos, s1, s2, seq)
            a = _swa_attention(qkv.reshape(batch, seq, -1), extra.astype(F32), batch, seq)
        else:
            wf = jnp.pad(wi[:, 3 * ATTN_WIDTH:], ((0, 0), (0, LANES - N_HEADS)))
            qkv, flog = _inproj_fox(x2d, row(an), wi, bf(wf))
            b_pad = jnp.pad(extra.astype(F32), (0, LANES - N_HEADS)).reshape(1, LANES)
            cum = _fox_cum(flog.reshape(batch, seq, LANES), b_pad, batch, seq)
            a = _fox_attention(qkv.reshape(batch, seq, -1),
                               cum.reshape(batch, N_HEADS, 1, seq), batch, seq)
        x2d = _tail(x2d, a.reshape(t, ATTN_WIDTH), p3d, i,
                    wo, row(mn), wu, bf(wd), row(pn), wg, wp, fg,
                    final=(i == len(layers) - 1))
    return x2d.reshape(batch, seq, d)
```
